```python
import jax, jax.numpy as jnp
from jax import lax
import numpy as np

D_MODEL = 1024
BATCH = 16
SEQ = 2048
DEPTH = 2

CONV_CH = D_MODEL
CONV_WIDTH = 31
HEAD_DIM = 64
N_Q_HEADS = D_MODEL // HEAD_DIM
N_KV_HEADS = N_Q_HEADS // 4
GQA_GROUP = N_Q_HEADS // N_KV_HEADS
ATTN_CH = N_Q_HEADS * HEAD_DIM
KV_CH = N_KV_HEADS * HEAD_DIM
WINDOW = 128
BLOCK = 128
SPAN = BLOCK + 2 * WINDOW
N_GROUPS = 4
EXPERTS_PER_GROUP = 4
TOP_K = 2
D_EXPERT = 512
D_PLE = 256
DN_ALPHA = (2 * DEPTH) ** 0.25
DN_BETA = (8 * DEPTH) ** -0.25
LN_EPS = 1e-5
NEG_INF = -1e30
IN_COLS = 2 * CONV_CH + ATTN_CH + 2 * KV_CH + 2 * D_MODEL
SPLITS = (2 * CONV_CH,
          2 * CONV_CH + ATTN_CH,
          2 * CONV_CH + ATTN_CH + KV_CH,
          2 * CONV_CH + ATTN_CH + 2 * KV_CH,
          2 * CONV_CH + ATTN_CH + 2 * KV_CH + D_MODEL)

kernel_name = 'hybrid_conformer_swa_hmoe_encoder'


def layer_norm(x, g, b):
    xf = x.astype(jnp.float32)
    mu = jnp.mean(xf, axis=-1, keepdims=True)
    var = jnp.mean(jnp.square(xf - mu), axis=-1, keepdims=True)
    y = (xf - mu) * lax.rsqrt(var + LN_EPS)
    return (y * g.astype(jnp.float32) + b.astype(jnp.float32)).astype(x.dtype)


def alibi_slopes(n_heads):
    return 2.0 ** (-8.0 * jnp.arange(1, n_heads + 1, dtype=jnp.float32) / n_heads)


def conformer_conv(u_glu, conv_w, conv_b, ln_g, ln_b):
    a, gate = jnp.split(u_glu, 2, axis=-1)
    u = a * jax.nn.sigmoid(gate)
    pad = CONV_WIDTH // 2
    y = lax.conv_general_dilated(u, conv_w[:, None, :].astype(u.dtype), (1,), [(pad, pad)],
                                 dimension_numbers=('NWC', 'WIO', 'NWC'),
                                 feature_group_count=CONV_CH)
    y = y + conv_b
    y = layer_norm(y, ln_g, ln_b)
    return jax.nn.silu(y)


def windowed_gqa(q, k, v, sink):
    B, S, _ = q.shape
    nb = S // BLOCK
    q = q.reshape(B, S, N_KV_HEADS, GQA_GROUP, HEAD_DIM) * (HEAD_DIM ** -0.5)
    k = k.reshape(B, S, N_KV_HEADS, HEAD_DIM)
    v = v.reshape(B, S, N_KV_HEADS, HEAD_DIM)
    kp = jnp.pad(k, ((0, 0), (WINDOW, WINDOW), (0, 0), (0, 0)))
    vp = jnp.pad(v, ((0, 0), (WINDOW, WINDOW), (0, 0), (0, 0)))
    slopes = alibi_slopes(N_Q_HEADS).reshape(N_KV_HEADS, GQA_GROUP)
    sink = sink.astype(jnp.float32).reshape(N_KV_HEADS, GQA_GROUP)[None, :, :, None]
    rel = jnp.arange(SPAN)[None, :] - WINDOW - jnp.arange(BLOCK)[:, None]
    in_window = jnp.abs(rel) <= WINDOW
    alibi = -slopes[:, :, None, None] * jnp.abs(rel).astype(jnp.float32)[None, None]

    def block(i):
        start = i * BLOCK
        qi = lax.dynamic_slice_in_dim(q, start, BLOCK, axis=1)
        ki = lax.dynamic_slice_in_dim(kp, start, SPAN, axis=1)
        vi = lax.dynamic_slice_in_dim(vp, start, SPAN, axis=1)
        key_pos = start - WINDOW + jnp.arange(SPAN)
        valid = in_window & ((key_pos >= 0) & (key_pos < S))[None, :]
        s = jnp.einsum('bqhgd,bkhd->bhgqk', qi, ki,
                       preferred_element_type=jnp.float32) + alibi
        s = jnp.where(valid, s, NEG_INF)
        m = jnp.maximum(jnp.max(s, axis=-1), sink)
        pr = jnp.exp(s - m[..., None])
        denom = jnp.sum(pr, axis=-1) + jnp.exp(sink - m)
        o = jnp.einsum('bhgqk,bkhd->bqhgd', pr.astype(vi.dtype), vi,
                       preferred_element_type=jnp.float32)
        o = o / jnp.transpose(denom, (0, 3, 1, 2))[..., None]
        return o.astype(q.dtype)

    out = lax.map(block, jnp.arange(nb))
    return jnp.moveaxis(out, 0, 1).reshape(B, S, ATTN_CH)


def hierarchical_moe(x, w_rg, b_rg, w_re, b_re, w1, w3, w2):
    B, S, D = x.shape
    xt = x.reshape(B * S, D)
    xf = xt.astype(jnp.float32)
    g_prob = jax.nn.softmax(xf @ w_rg.astype(jnp.float32) + b_rg.astype(jnp.float32), axis=-1)
    g_w, g_idx = lax.top_k(g_prob, 1)
    e_logits = (xf @ w_re.astype(jnp.float32) + b_re.astype(jnp.float32)).reshape(-1, N_GROUPS, EXPERTS_PER_GROUP)
    e_sel = jnp.take_along_axis(e_logits, g_idx[:, :, None], axis=1)[:, 0]
    e_top, e_idx = lax.top_k(e_sel, TOP_K)
    e_w = jax.nn.softmax(e_top, axis=-1) * g_w
    comb_e = jnp.einsum('tk,tke->te', e_w, jax.nn.one_hot(e_idx, EXPERTS_PER_GROUP, dtype=jnp.float32))
    comb = (jax.nn.one_hot(g_idx[:, 0], N_GROUPS, dtype=jnp.float32)[:, :, None]
            * comb_e[:, None, :]).astype(x.dtype)
    y = jnp.zeros_like(xt)
    for g in range(N_GROUPS):
        h = jax.nn.silu(jnp.einsum('td,edf->tef', xt, w1[g])) * jnp.einsum('td,edf->tef', xt, w3[g])
        y = y + jnp.einsum('tef,efd->td', h * comb[:, g, :, None], w2[g])
    return y.reshape(B, S, D)


def _normal(key, shape, scale):
    return jax.random.normal(key, shape, jnp.float32) * scale


def setup_inputs(seed: int = 0) -> dict:
    key = jax.random.key(seed)
    ks = jax.random.split(key, 32)
    L, D, C, F = DEPTH, D_MODEL, CONV_CH, D_EXPERT
    G, E = N_GROUPS, EXPERTS_PER_GROUP
    return {
        'x': _normal(ks[0], (BATCH, SEQ, D), 1.0),
        'p': _normal(ks[1], (DEPTH, BATCH, SEQ, D_PLE), 1.0),
        'ln_emb_g': 1.0 + _normal(ks[2], (D,), 0.02),
        'ln_emb_b': _normal(ks[3], (D,), 0.02),
        'w_in': _normal(ks[4], (L, D, IN_COLS), D ** -0.5),
        'b_in': _normal(ks[5], (L, IN_COLS), 0.02),
        'conv_w': _normal(ks[6], (L, CONV_WIDTH, C), CONV_WIDTH ** -0.5),
        'conv_b': _normal(ks[7], (L, C), 0.02),
        'conv_ln_g': 1.0 + _normal(ks[8], (L, C), 0.02),
        'conv_ln_b': _normal(ks[9], (L, C), 0.02),
        'w_conv_out': _normal(ks[10], (L, C, D), C ** -0.5 * DN_BETA),
        'w_attn_out': _normal(ks[11], (L, ATTN_CH, D), ATTN_CH ** -0.5 * DN_BETA),
        'attn_sink': _normal(ks[12], (L, N_Q_HEADS), 0.5),
        'w_out': _normal(ks[13], (L, D, D), D ** -0.5 * DN_BETA),
        'ln1_g': 1.0 + _normal(ks[14], (L, D), 0.02),
        'ln1_b': _normal(ks[15], (L, D), 0.02),
        'w_router_group': _normal(ks[16], (L, D, G), D ** -0.5),
        'b_router_group': _normal(ks[17], (L, G), 0.01),
        'w_router_expert': _normal(ks[18], (L, D, G * E), D ** -0.5),
        'b_router_expert': _normal(ks[19], (L, G * E), 0.01),
        'w1': _normal(ks[20], (L, G, E, D, F), D ** -0.5),
        'w3': _normal(ks[21], (L, G, E, D, F), D ** -0.5),
        'w2': _normal(ks[22], (L, G, E, F, D), F ** -0.5 * DN_BETA),
        'w_p': _normal(ks[23], (L, D_PLE, D), D_PLE ** -0.5 * DN_BETA),
        'w_pg': _normal(ks[24], (L, D, D), D ** -0.5),
        'b_pg': _normal(ks[25], (L, D), 0.02),
        'ln2_g': 1.0 + _normal(ks[26], (L, D), 0.02),
        'ln2_b': _normal(ks[27], (L, D), 0.02),
    }


def reference(x, p, ln_emb_g, ln_emb_b, w_in, b_in, conv_w, conv_b, conv_ln_g, conv_ln_b,
              w_conv_out, w_attn_out, attn_sink, w_out, ln1_g, ln1_b,
              w_router_group, b_router_group, w_router_expert, b_router_expert,
              w1, w3, w2, w_p, w_pg, b_pg, ln2_g, ln2_b):
    x = layer_norm(x, ln_emb_g, ln_emb_b)
    for i in range(DEPTH):
        proj = x @ w_in[i] + b_in[i]
        u_glu, q, k, v, g_conv, g_attn = jnp.split(proj, SPLITS, axis=-1)
        y_conv = conformer_conv(u_glu, conv_w[i], conv_b[i], conv_ln_g[i], conv_ln_b[i]) @ w_conv_out[i]
        y_attn = windowed_gqa(q, k, v, attn_sink[i]) @ w_attn_out[i]
        merged = jax.nn.sigmoid(g_conv) * y_conv + jax.nn.sigmoid(g_attn) * y_attn
        x = layer_norm(DN_ALPHA * x + merged @ w_out[i], ln1_g[i], ln1_b[i])
        ffn = hierarchical_moe(x, w_router_group[i], b_router_group[i], w_router_expert[i],
                               b_router_expert[i], w1[i], w3[i], w2[i])
        ple = jax.nn.sigmoid(x @ w_pg[i] + b_pg[i]) * (p[i] @ w_p[i])
        x = layer_norm(DN_ALPHA * x + ffn + ple, ln2_g[i], ln2_b[i])
    return x
```

```python
import functools

import jax
import jax.numpy as jnp
from jax import lax
from jax.experimental import pallas as pl
from jax.experimental.pallas import tpu as pltpu

D_MODEL = 1024
BATCH = 16
SEQ = 2048
DEPTH = 2
TOKENS = BATCH * SEQ
CONV_CH = D_MODEL
CONV_WIDTH = 31
CONV_PAD = CONV_WIDTH // 2
HEAD_DIM = 64
N_Q_HEADS = 16
N_KV_HEADS = 4
GQA_GROUP = 4
KV_CH = N_KV_HEADS * HEAD_DIM
WINDOW = 128
BLOCK = 128
SPAN = BLOCK + 2 * WINDOW
N_GROUPS = 4
EXPERTS_PER_GROUP = 4
N_EXPERTS = N_GROUPS * EXPERTS_PER_GROUP
D_EXPERT = 512
D_PLE = 256
DN_ALPHA = (2 * DEPTH) ** 0.25
LN_EPS = 1e-5
NEG_INF = -1e30

LANES = 128
N_LANE_CHUNKS = D_MODEL // LANES
HALO = 16
VMEM_LIMIT = 56 * 1024 * 1024

F32 = jnp.float32
BF16 = jnp.bfloat16


def _layer_norm(v, g, b):
    mu = jnp.mean(v, axis=-1, keepdims=True)
    d = v - mu
    var = jnp.mean(d * d, axis=-1, keepdims=True)
    return d * lax.rsqrt(var + LN_EPS) * g + b


def _cparams(*sem):
    return pltpu.CompilerParams(dimension_semantics=sem, vmem_limit_bytes=VMEM_LIMIT)


def _ln_kernel(x_ref, g_ref, b_ref, o_ref):
    o_ref[...] = _layer_norm(x_ref[...], g_ref[...], b_ref[...])


def _ln_call(x, g, b, tm=512):
    return pl.pallas_call(
        _ln_kernel,
        grid=(TOKENS // tm,),
        in_specs=[pl.BlockSpec((tm, D_MODEL), lambda i: (i, 0)),
                  pl.BlockSpec((1, D_MODEL), lambda i: (0, 0)),
                  pl.BlockSpec((1, D_MODEL), lambda i: (0, 0))],
        out_specs=pl.BlockSpec((tm, D_MODEL), lambda i: (i, 0)),
        out_shape=jax.ShapeDtypeStruct((TOKENS, D_MODEL), F32),
        compiler_params=_cparams("arbitrary"),
        name="ln_emb",
    )(x, g, b)


def _glu_kernel(x_ref, wa_ref, wg_ref, ba_ref, bg_ref, u_ref):
    xb = x_ref[...].astype(BF16)
    a = jnp.dot(xb, wa_ref[...], preferred_element_type=F32) + ba_ref[...]
    g = jnp.dot(xb, wg_ref[...], preferred_element_type=F32) + bg_ref[...]
    u_ref[...] = (a * jax.nn.sigmoid(g)).astype(BF16)


def _glu_call(x, wa, wg, ba, bg, tm=1024, bn=512):
    return pl.pallas_call(
        _glu_kernel,
        grid=(TOKENS // tm, CONV_CH // bn),
        in_specs=[pl.BlockSpec((tm, D_MODEL), lambda i, j: (i, 0)),
                  pl.BlockSpec((D_MODEL, bn), lambda i, j: (0, j)),
                  pl.BlockSpec((D_MODEL, bn), lambda i, j: (0, j)),
                  pl.BlockSpec((1, bn), lambda i, j: (0, j)),
                  pl.BlockSpec((1, bn), lambda i, j: (0, j))],
        out_specs=pl.BlockSpec((tm, bn), lambda i, j: (i, j)),
        out_shape=jax.ShapeDtypeStruct((TOKENS, CONV_CH), BF16),
        compiler_params=_cparams("arbitrary", "arbitrary"),
        name="glu_proj",
    )(x, wa, wg, ba, bg)


def _proj_kernel(x_ref, w_ref, b_ref, o_ref):
    acc = jnp.dot(x_ref[...].astype(BF16), w_ref[...], preferred_element_type=F32)
    o_ref[...] = (acc + b_ref[...]).astype(BF16)


def _kv_call(x, w, b, tm=1024, bn=512):
    n = w.shape[1]
    return pl.pallas_call(
        _proj_kernel,
        grid=(TOKENS // tm, n // bn),
        in_specs=[pl.BlockSpec((tm, D_MODEL), lambda i, j: (i, 0)),
                  pl.BlockSpec((D_MODEL, bn), lambda i, j: (0, j)),
                  pl.BlockSpec((1, bn), lambda i, j: (0, j))],
        out_specs=pl.BlockSpec((tm, bn), lambda i, j: (i, j)),
        out_shape=jax.ShapeDtypeStruct((TOKENS, n), BF16),
        compiler_params=_cparams("arbitrary", "arbitrary"),
        name="kv_proj",
    )(x, w, b)


MIX_TS = 256
CONV_RB = 64
N_VARIANTS = 3


def _alibi_slope(h):
    return 2.0 ** (-8.0 * (h + 1) / N_Q_HEADS)


def _mixer_kernel(sink_ref,
                  x_ref, u_ref, kv_ref, wq_ref, bq_ref, wg_ref, bg_ref,
                  cw_ref, cb_ref, clg_ref, clb_ref,
                  wco_ref, wao_ref, wo_ref, l1g_ref, l1b_ref,
                  o_ref,
                  ubuf, ybuf, obuf, bias_tab):
    b = pl.program_id(0)
    s = pl.program_id(1)
    ts = MIX_TS
    t0 = pl.multiple_of(s * ts, ts)

    @pl.when((b == 0) & (s == 0))
    def _():
        r = lax.broadcasted_iota(jnp.int32, (BLOCK, SPAN), 0)
        kk = lax.broadcasted_iota(jnp.int32, (BLOCK, SPAN), 1)
        for v in range(N_VARIANTS):
            dist = jnp.abs(kk - r - v * WINDOW)
            inside = dist <= WINDOW
            distf = dist.astype(F32)
            for h in range(N_Q_HEADS):
                bias_tab[v * N_Q_HEADS + h] = jnp.where(inside, -_alibi_slope(h) * distf, NEG_INF)

    lo = pl.multiple_of(jnp.maximum(t0 - HALO, 0), HALO)
    hi = pl.multiple_of(jnp.minimum(t0 + ts, SEQ - HALO), HALO)
    has_lo = t0 > 0
    has_hi = t0 + ts < SEQ
    for k in range(N_LANE_CHUNKS):
        cs = slice(LANES * k, LANES * (k + 1))
        ubuf[k, HALO:HALO + ts, :] = u_ref[pl.ds(t0, ts), cs].astype(F32)
        ubuf[k, 0:HALO, :] = jnp.where(has_lo, u_ref[pl.ds(lo, HALO), cs].astype(F32), 0.0)
        ubuf[k, HALO + ts:2 * HALO + ts, :] = jnp.where(has_hi, u_ref[pl.ds(hi, HALO), cs].astype(F32), 0.0)

    base = HALO - CONV_PAD
    for k in range(N_LANE_CHUNKS):
        cs = slice(LANES * k, LANES * (k + 1))
        wk = cw_ref[k]
        for rb in range(ts // CONV_RB):
            r0 = rb * CONV_RB
            acc = jnp.broadcast_to(cb_ref[:, cs], (CONV_RB, LANES))
            for j in range(CONV_WIDTH):
                acc = acc + ubuf[k, r0 + base + j:r0 + base + j + CONV_RB, :] * wk[j:j + 1, :]
            ybuf[r0:r0 + CONV_RB, cs] = acc
    yn = _layer_norm(ybuf[...], clg_ref[...], clb_ref[...])
    act = (yn * jax.nn.sigmoid(yn)).astype(BF16)
    y_conv = jnp.dot(act, wco_ref[...], preferred_element_type=F32)

    xt = x_ref[...]
    xb = xt.astype(BF16)
    q = (jnp.dot(xb, wq_ref[...], preferred_element_type=F32) + bq_ref[...]).astype(BF16)
    low_half = lax.broadcasted_iota(jnp.int32, (BLOCK, LANES), 1) < HEAD_DIM
    zero_q = jnp.zeros((BLOCK, LANES), BF16)
    nt_dims = (((1,), (1,)), ((), ()))
    for qi in range(ts // BLOCK):
        rows = slice(BLOCK * qi, BLOCK * (qi + 1))
        qs = t0 + BLOCK * qi
        ws = pl.multiple_of(jnp.clip(qs - WINDOW, 0, SEQ - SPAN), BLOCK)
        variant = jnp.where(qs == 0, 0, jnp.where(qs == SEQ - BLOCK, 2, 1))
        for h in range(N_KV_HEADS):
            kw = kv_ref[pl.ds(ws, SPAN), LANES * h:LANES * (h + 1)]
            vw = kv_ref[pl.ds(ws, SPAN), 2 * KV_CH + LANES * h:2 * KV_CH + LANES * (h + 1)]
            qc = [q[rows, 2 * LANES * h:2 * LANES * h + LANES],
                  q[rows, 2 * LANES * h + LANES:2 * LANES * (h + 1)]]
            qst = jnp.concatenate([jnp.where(low_half, qc[0], zero_q), jnp.where(low_half, zero_q, qc[0]),
                                   jnp.where(low_half, qc[1], zero_q), jnp.where(low_half, zero_q, qc[1])], axis=0)
            sc = lax.dot_general(qst, kw, nt_dims, preferred_element_type=F32)
            probs = []
            inv_den = []
            for g in range(GQA_GROUP):
                hq = GQA_GROUP * h + g
                sg = sc[BLOCK * g:BLOCK * (g + 1), :] + bias_tab[variant * N_Q_HEADS + hq]
                sink = sink_ref[hq]
                m = jnp.maximum(jnp.max(sg, axis=-1, keepdims=True), sink)
                p = jnp.exp(sg - m)
                den = jnp.sum(p, axis=-1, keepdims=True) + jnp.exp(sink - m)
                probs.append(p.astype(BF16))
                inv_den.append(den)
            pv = jnp.dot(jnp.concatenate(probs, axis=0), vw, preferred_element_type=F32)
            og = [pv[BLOCK * g:BLOCK * (g + 1), :] / inv_den[g] for g in range(GQA_GROUP)]
            obuf[rows, 2 * LANES * h:2 * LANES * h + LANES] = jnp.where(low_half, og[0], og[1]).astype(BF16)
            obuf[rows, 2 * LANES * h + LANES:2 * LANES * (h + 1)] = jnp.where(low_half, og[2], og[3]).astype(BF16)
    y_attn = jnp.dot(obuf[...], wao_ref[...], preferred_element_type=F32)

    gates = jax.nn.sigmoid(jnp.dot(xb, wg_ref[...], preferred_element_type=F32) + bg_ref[...])
    merged = gates[:, :D_MODEL] * y_conv + gates[:, D_MODEL:] * y_attn
    out = jnp.dot(merged.astype(BF16), wo_ref[...], preferred_element_type=F32)
    o_ref[...] = _layer_norm(DN_ALPHA * xt + out, l1g_ref[...], l1b_ref[...])


def _mixer_call(sink, x, u, kv, wq, bq, wg, bg, cw, cb, clg, clb, wco, wao, wo, l1g, l1b):
    ts = MIX_TS
    ns = SEQ // ts
    const = lambda *shape: pl.BlockSpec(shape, lambda b, s, sk: (0,) * len(shape),
                                        pipeline_mode=pl.Buffered(1))
    grid_spec = pltpu.PrefetchScalarGridSpec(
        num_scalar_prefetch=1,
        grid=(BATCH, ns),
        in_specs=[pl.BlockSpec((ts, D_MODEL), lambda b, s, sk: (b * ns + s, 0)),
                  pl.BlockSpec((None, SEQ, CONV_CH), lambda b, s, sk: (b, 0, 0)),
                  pl.BlockSpec((None, SEQ, 4 * KV_CH), lambda b, s, sk: (b, 0, 0)),
                  const(D_MODEL, D_MODEL), const(1, D_MODEL),
                  const(D_MODEL, 2 * D_MODEL), const(1, 2 * D_MODEL),
                  const(N_LANE_CHUNKS, 32, LANES), const(1, CONV_CH), const(1, CONV_CH), const(1, CONV_CH),
                  const(CONV_CH, D_MODEL), const(D_MODEL, D_MODEL), const(D_MODEL, D_MODEL),
                  const(1, D_MODEL), const(1, D_MODEL)],
        out_specs=pl.BlockSpec((ts, D_MODEL), lambda b, s, sk: (b * ns + s, 0)),
        scratch_shapes=[pltpu.VMEM((N_LANE_CHUNKS, ts + 2 * HALO, LANES), F32),
                        pltpu.VMEM((ts, CONV_CH), F32),
                        pltpu.VMEM((ts, D_MODEL), BF16),
                        pltpu.VMEM((N_VARIANTS * N_Q_HEADS, BLOCK, SPAN), F32)],
    )
    return pl.pallas_call(
        _mixer_kernel,
        grid_spec=grid_spec,
        out_shape=jax.ShapeDtypeStruct((TOKENS, D_MODEL), F32),
        compiler_params=_cparams("arbitrary", "arbitrary"),
        name="mixer",
    )(sink, x, u, kv, wq, bq, wg, bg, cw, cb, clg, clb, wco, wao, wo, l1g, l1b)


ROUTER_ROWS = 32


def _router_kernel(x_ref, w_ref, b_ref, comb_ref):
    x = x_ref[...]
    tm = x.shape[0]
    xh = x.astype(BF16)
    xl = (x - xh.astype(F32)).astype(BF16)
    nt_dims = (((1,), (1,)), ((), ()))
    lt = (lax.dot_general(w_ref[0], xh, nt_dims, preferred_element_type=F32)
          + lax.dot_general(w_ref[1], xh, nt_dims, preferred_element_type=F32)
          + lax.dot_general(w_ref[0], xl, nt_dims, preferred_element_type=F32))
    lt = lt + b_ref[...]
    row4 = lax.broadcasted_iota(jnp.int32, (N_GROUPS, tm), 0)
    gl = lt[0:N_GROUPS, :]
    gmax = jnp.max(gl, axis=0, keepdims=True)
    g_w = 1.0 / jnp.sum(jnp.exp(gl - gmax), axis=0, keepdims=True)
    g_idx = jnp.min(jnp.where(gl == gmax, row4, N_GROUPS), axis=0, keepdims=True)
    e_sel = jnp.zeros((EXPERTS_PER_GROUP, tm), F32)
    for g in range(N_GROUPS):
        lo = N_GROUPS + EXPERTS_PER_GROUP * g
        e_sel = e_sel + jnp.where(g_idx == g, lt[lo:lo + EXPERTS_PER_GROUP, :], 0.0)
    e1 = jnp.max(e_sel, axis=0, keepdims=True)
    i1 = jnp.min(jnp.where(e_sel == e1, row4, EXPERTS_PER_GROUP), axis=0, keepdims=True)
    rest = jnp.where(row4 == i1, -jnp.inf, e_sel)
    e2 = jnp.max(rest, axis=0, keepdims=True)
    i2 = jnp.min(jnp.where(rest == e2, row4, EXPERTS_PER_GROUP), axis=0, keepdims=True)
    t = jnp.exp(e2 - e1)
    w1 = (1.0 / (1.0 + t)) * g_w
    w2 = (t / (1.0 + t)) * g_w
    comb_e = jnp.where(row4 == i1, w1, 0.0) + jnp.where(row4 == i2, w2, 0.0)
    rows = [jnp.where(g_idx == g, comb_e, 0.0) for g in range(N_GROUPS)]
    rows.append(jnp.zeros((LANES - N_EXPERTS, tm), F32))
    comb_ref[...] = jnp.concatenate(rows, axis=0).T


def _router_call(x, w, b, tm=512):
    return pl.pallas_call(
        _router_kernel,
        grid=(TOKENS // tm,),
        in_specs=[pl.BlockSpec((tm, D_MODEL), lambda i: (i, 0)),
                  pl.BlockSpec((2, ROUTER_ROWS, D_MODEL), lambda i: (0, 0, 0)),
                  pl.BlockSpec((ROUTER_ROWS, tm), lambda i: (0, 0))],
        out_specs=pl.BlockSpec((tm, LANES), lambda i: (i, 0)),
        out_shape=jax.ShapeDtypeStruct((TOKENS, LANES), F32),
        compiler_params=_cparams("arbitrary"),
        name="router",
    )(x, w, b)


def _moe_kernel(x_ref, comb_ref, w1_ref, w3_ref, w2_ref, o_ref, acc_ref):
    e = pl.program_id(1)

    @pl.when(e == 0)
    def _():
        acc_ref[...] = jnp.zeros_like(acc_ref)

    xb = x_ref[...].astype(BF16)
    a = jnp.dot(xb, w1_ref[...], preferred_element_type=F32)
    c = jnp.dot(xb, w3_ref[...], preferred_element_type=F32)
    lane = lax.broadcasted_iota(jnp.int32, comb_ref.shape, 1)
    wgt = jnp.sum(jnp.where(lane == e, comb_ref[...], 0.0), axis=-1, keepdims=True)
    h = (a * jax.nn.sigmoid(a)) * c * wgt
    acc_ref[...] += jnp.dot(h.astype(BF16), w2_ref[...], preferred_element_type=F32)

    @pl.when(e == N_EXPERTS - 1)
    def _():
        o_ref[...] = acc_ref[...]


def _moe_call(x, comb, w1, w3, w2, tm=1024):
    return pl.pallas_call(
        _moe_kernel,
        grid=(TOKENS // tm, N_EXPERTS),
        in_specs=[pl.BlockSpec((tm, D_MODEL), lambda i, e: (i, 0)),
                  pl.BlockSpec((tm, LANES), lambda i, e: (i, 0)),
                  pl.BlockSpec((None, D_MODEL, D_EXPERT), lambda i, e: (e, 0, 0)),
                  pl.BlockSpec((None, D_MODEL, D_EXPERT), lambda i, e: (e, 0, 0)),
                  pl.BlockSpec((None, D_EXPERT, D_MODEL), lambda i, e: (e, 0, 0))],
        out_specs=pl.BlockSpec((tm, D_MODEL), lambda i, e: (i, 0)),
        out_shape=jax.ShapeDtypeStruct((TOKENS, D_MODEL), F32),
        scratch_shapes=[pltpu.VMEM((tm, D_MODEL), F32)],
        compiler_params=_cparams("arbitrary", "arbitrary"),
        name="moe",
    )(x, comb, w1, w3, w2)


def _ple_ln_kernel(x_ref, f_ref, p_ref, wpg_ref, bpg_ref, wp_ref, g_ref, b_ref, o_ref):
    x = x_ref[...]
    gate = jax.nn.sigmoid(jnp.dot(x.astype(BF16), wpg_ref[...], preferred_element_type=F32) + bpg_ref[...])
    pe = jnp.dot(p_ref[...].astype(BF16), wp_ref[...], preferred_element_type=F32)
    o_ref[...] = _layer_norm(DN_ALPHA * x + f_ref[...] + gate * pe, g_ref[...], b_ref[...])


def _ple_ln_call(x, f, p, wpg, bpg, wp, g, b, tm=512):
    row = lambda n: pl.BlockSpec((tm, n), lambda i: (i, 0))
    const = lambda *shape: pl.BlockSpec(shape, lambda i: (0,) * len(shape))
    return pl.pallas_call(
        _ple_ln_kernel,
        grid=(TOKENS // tm,),
        in_specs=[row(D_MODEL), row(D_MODEL), row(D_PLE),
                  const(D_MODEL, D_MODEL), const(1, D_MODEL), const(D_PLE, D_MODEL),
                  const(1, D_MODEL), const(1, D_MODEL)],
        out_specs=row(D_MODEL),
        out_shape=jax.ShapeDtypeStruct((TOKENS, D_MODEL), F32),
        compiler_params=_cparams("arbitrary"),
        name="ple_ln",
    )(x, f, p, wpg, bpg, wp, g, b)


def _dup_heads(w):
    lead = w.shape[:-1]
    w4 = w.reshape(lead + (N_KV_HEADS, 1, HEAD_DIM))
    return jnp.broadcast_to(w4, lead + (N_KV_HEADS, 2, HEAD_DIM)).reshape(lead + (2 * KV_CH,))


def _row(v):
    return v.reshape(1, -1)


def kernel(x, p, ln_emb_g, ln_emb_b, w_in, b_in, conv_w, conv_b, conv_ln_g, conv_ln_b, w_conv_out, w_attn_out, attn_sink, w_out, ln1_g, ln1_b, w_router_group, b_router_group, w_router_expert, b_router_expert, w1, w3, w2, w_p, w_pg, b_pg, ln2_g, ln2_b):
    c0, c1, c2, c3, c4 = (2 * CONV_CH, 2 * CONV_CH + D_MODEL, 2 * CONV_CH + D_MODEL + KV_CH,
                          2 * CONV_CH + D_MODEL + 2 * KV_CH, 2 * CONV_CH + 2 * D_MODEL + 2 * KV_CH)
    q_scale = HEAD_DIM ** -0.5
    router_tm = 512

    xs = _ln_call(x.reshape(TOKENS, D_MODEL), _row(ln_emb_g), _row(ln_emb_b))
    for i in range(DEPTH):
        w, b = w_in[i], b_in[i]
        wa = w[:, :CONV_CH].astype(BF16)
        wgl = w[:, CONV_CH:c0].astype(BF16)
        wq = (w[:, c0:c1] * q_scale).astype(BF16)
        bq = _row(b[c0:c1] * q_scale)
        wkv = jnp.concatenate([_dup_heads(w[:, c1:c2]), _dup_heads(w[:, c2:c3])], axis=1).astype(BF16)
        bkv = _row(jnp.concatenate([_dup_heads(b[c1:c2]), _dup_heads(b[c2:c3])]))
        wgate = w[:, c3:].astype(BF16)
        bgate = _row(b[c3:])
        cw = jnp.pad(conv_w[i], ((0, 1), (0, 0))).reshape(32, N_LANE_CHUNKS, LANES).transpose(1, 0, 2)

        u = _glu_call(xs, wa, wgl, _row(b[:CONV_CH]), _row(b[CONV_CH:c0]))
        kv = _kv_call(xs, wkv, bkv)
        x1 = _mixer_call(attn_sink[i].astype(F32), xs,
                         u.reshape(BATCH, SEQ, CONV_CH), kv.reshape(BATCH, SEQ, 4 * KV_CH),
                         wq, bq, wgate, bgate, cw, _row(conv_b[i]), _row(conv_ln_g[i]), _row(conv_ln_b[i]),
                         w_conv_out[i].astype(BF16), w_attn_out[i].astype(BF16), w_out[i].astype(BF16),
                         _row(ln1_g[i]), _row(ln1_b[i]))

        wr = jnp.concatenate([w_router_group[i], w_router_expert[i]], axis=1).T
        wr = jnp.pad(wr, ((0, ROUTER_ROWS - wr.shape[0]), (0, 0)))
        wr_hi = wr.astype(BF16)
        wr_lo = (wr - wr_hi.astype(F32)).astype(BF16)
        br = jnp.pad(jnp.concatenate([b_router_group[i], b_router_expert[i]]), (0, ROUTER_ROWS - 20))
        br = jnp.broadcast_to(br[:, None], (ROUTER_ROWS, router_tm))
        comb = _router_call(x1, jnp.stack([wr_hi, wr_lo]), br, tm=router_tm)

        ffn = _moe_call(x1, comb,
                        w1[i].reshape(N_EXPERTS, D_MODEL, D_EXPERT).astype(BF16),
                        w3[i].reshape(N_EXPERTS, D_MODEL, D_EXPERT).astype(BF16),
                        w2[i].reshape(N_EXPERTS, D_EXPERT, D_MODEL).astype(BF16))
        xs = _ple_ln_call(x1, ffn, p[i].reshape(TOKENS, D_PLE), w_pg[i].astype(BF16), _row(b_pg[i]),
                          w_p[i].astype(BF16), _row(ln2_g[i]), _row(ln2_b[i]))
    return xs.reshape(BATCH, SEQ, D_MODEL)
```

```python
import functools

import jax
import jax.numpy as jnp
from jax import lax
from jax.experimental import pallas as pl
from jax.experimental.pallas import tpu as pltpu

D_MODEL = 1024
BATCH = 16
SEQ = 2048
DEPTH = 2
TOKENS = BATCH * SEQ
CONV_CH = D_MODEL
CONV_WIDTH = 31
CONV_PAD = CONV_WIDTH // 2
HEAD_DIM = 64
N_Q_HEADS = 16
N_KV_HEADS = 4
GQA_GROUP = 4
KV_CH = N_KV_HEADS * HEAD_DIM
WINDOW = 128
BLOCK = 128
SPAN = BLOCK + 2 * WINDOW
N_GROUPS = 4
EXPERTS_PER_GROUP = 4
N_EXPERTS = N_GROUPS * EXPERTS_PER_GROUP
D_EXPERT = 512
D_PLE = 256
DN_ALPHA = (2 * DEPTH) ** 0.25
LN_EPS = 1e-5
NEG_INF = -1e30

LANES = 128
N_LANE_CHUNKS = D_MODEL // LANES
HALO = 16
VMEM_LIMIT = 56 * 1024 * 1024

F32 = jnp.float32
BF16 = jnp.bfloat16


def _layer_norm(v, g, b):
    mu = jnp.mean(v, axis=-1, keepdims=True)
    d = v - mu
    var = jnp.mean(d * d, axis=-1, keepdims=True)
    return d * lax.rsqrt(var + LN_EPS) * g + b


def _cparams(*sem):
    return pltpu.CompilerParams(dimension_semantics=sem, vmem_limit_bytes=VMEM_LIMIT)


def _ln_kernel(x_ref, g_ref, b_ref, o_ref):
    o_ref[...] = _layer_norm(x_ref[...], g_ref[...], b_ref[...])


def _ln_call(x, g, b, tm=512):
    return pl.pallas_call(
        _ln_kernel,
        grid=(TOKENS // tm,),
        in_specs=[pl.BlockSpec((tm, D_MODEL), lambda i: (i, 0)),
                  pl.BlockSpec((1, D_MODEL), lambda i: (0, 0)),
                  pl.BlockSpec((1, D_MODEL), lambda i: (0, 0))],
        out_specs=pl.BlockSpec((tm, D_MODEL), lambda i: (i, 0)),
        out_shape=jax.ShapeDtypeStruct((TOKENS, D_MODEL), F32),
        compiler_params=_cparams("arbitrary"),
        name="ln_emb",
    )(x, g, b)


def _glu_kernel(x_ref, wa_ref, wg_ref, ba_ref, bg_ref, u_ref):
    xb = x_ref[...].astype(BF16)
    a = jnp.dot(xb, wa_ref[...], preferred_element_type=F32) + ba_ref[...]
    g = jnp.dot(xb, wg_ref[...], preferred_element_type=F32) + bg_ref[...]
    u_ref[...] = (a * jax.nn.sigmoid(g)).astype(BF16)


def _glu_call(x, wa, wg, ba, bg, tm=1024, bn=512):
    return pl.pallas_call(
        _glu_kernel,
        grid=(TOKENS // tm, CONV_CH // bn),
        in_specs=[pl.BlockSpec((tm, D_MODEL), lambda i, j: (i, 0)),
                  pl.BlockSpec((D_MODEL, bn), lambda i, j: (0, j)),
                  pl.BlockSpec((D_MODEL, bn), lambda i, j: (0, j)),
                  pl.BlockSpec((1, bn), lambda i, j: (0, j)),
                  pl.BlockSpec((1, bn), lambda i, j: (0, j))],
        out_specs=pl.BlockSpec((tm, bn), lambda i, j: (i, j)),
        out_shape=jax.ShapeDtypeStruct((TOKENS, CONV_CH), BF16),
        compiler_params=_cparams("arbitrary", "arbitrary"),
        name="glu_proj",
    )(x, wa, wg, ba, bg)


def _proj_kernel(x_ref, w_ref, b_ref, o_ref):
    acc = jnp.dot(x_ref[...].astype(BF16), w_ref[...], preferred_element_type=F32)
    o_ref[...] = (acc + b_ref[...]).astype(BF16)


def _kv_call(x, w, b, tm=1024, bn=512):
    n = w.shape[1]
    return pl.pallas_call(
        _proj_kernel,
        grid=(TOKENS // tm, n // bn),
        in_specs=[pl.BlockSpec((tm, D_MODEL), lambda i, j: (i, 0)),
                  pl.BlockSpec((D_MODEL, bn), lambda i, j: (0, j)),
                  pl.BlockSpec((1, bn), lambda i, j: (0, j))],
        out_specs=pl.BlockSpec((tm, bn), lambda i, j: (i, j)),
        out_shape=jax.ShapeDtypeStruct((TOKENS, n), BF16),
        compiler_params=_cparams("arbitrary", "arbitrary"),
        name="kv_proj",
    )(x, w, b)


MIX_TS = 256
CONV_RB = 64
N_VARIANTS = 3


def _alibi_slope(h):
    return 2.0 ** (-8.0 * (h + 1) / N_Q_HEADS)


def _mixer_kernel(sink_ref,
                  x_ref, u_ref, kv_ref, wq_ref, bq_ref, wg_ref, bg_ref,
                  cw_ref, cb_ref, clg_ref, clb_ref,
                  wco_ref, wao_ref, wo_ref, l1g_ref, l1b_ref,
                  o_ref,
                  ubuf, ybuf, obuf, bias_tab):
    b = pl.program_id(0)
    s = pl.program_id(1)
    ts = MIX_TS
    t0 = pl.multiple_of(s * ts, ts)

    @pl.when((b == 0) & (s == 0))
    def _():
        r = lax.broadcasted_iota(jnp.int32, (BLOCK, SPAN), 0)
        kk = lax.broadcasted_iota(jnp.int32, (BLOCK, SPAN), 1)
        for v in range(N_VARIANTS):
            dist = jnp.abs(kk - r - v * WINDOW)
            inside = dist <= WINDOW
            distf = dist.astype(F32)
            for h in range(N_Q_HEADS):
                bias_tab[v * N_Q_HEADS + h] = jnp.where(inside, -_alibi_slope(h) * distf, NEG_INF)

    lo = pl.multiple_of(jnp.maximum(t0 - HALO, 0), HALO)
    hi = pl.multiple_of(jnp.minimum(t0 + ts, SEQ - HALO), HALO)
    has_lo = t0 > 0
    has_hi = t0 + ts < SEQ
    for k in range(N_LANE_CHUNKS):
        cs = slice(LANES * k, LANES * (k + 1))
        ubuf[k, HALO:HALO + ts, :] = u_ref[pl.ds(t0, ts), cs].astype(F32)
        ubuf[k, 0:HALO, :] = jnp.where(has_lo, u_ref[pl.ds(lo, HALO), cs].astype(F32), 0.0)
        ubuf[k, HALO + ts:2 * HALO + ts, :] = jnp.where(has_hi, u_ref[pl.ds(hi, HALO), cs].astype(F32), 0.0)

    base = HALO - CONV_PAD
    for k in range(N_LANE_CHUNKS):
        cs = slice(LANES * k, LANES * (k + 1))
        wk = cw_ref[k]
        for rb in range(ts // CONV_RB):
            r0 = rb * CONV_RB
            acc = jnp.broadcast_to(cb_ref[:, cs], (CONV_RB, LANES))
            for j in range(CONV_WIDTH):
                acc = acc + ubuf[k, r0 + base + j:r0 + base + j + CONV_RB, :] * wk[j:j + 1, :]
            ybuf[r0:r0 + CONV_RB, cs] = acc
    yn = _layer_norm(ybuf[...], clg_ref[...], clb_ref[...])
    act = (yn * jax.nn.sigmoid(yn)).astype(BF16)
    y_conv = jnp.dot(act, wco_ref[...], preferred_element_type=F32)

    xt = x_ref[...]
    xb = xt.astype(BF16)
    q = (jnp.dot(xb, wq_ref[...], preferred_element_type=F32) + bq_ref[...]).astype(BF16)
    low_half = lax.broadcasted_iota(jnp.int32, (BLOCK, LANES), 1) < HEAD_DIM
    zero_q = jnp.zeros((BLOCK, LANES), BF16)
    nt_dims = (((1,), (1,)), ((), ()))
    for qi in range(ts // BLOCK):
        rows = slice(BLOCK * qi, BLOCK * (qi + 1))
        qs = t0 + BLOCK * qi
        ws = pl.multiple_of(jnp.clip(qs - WINDOW, 0, SEQ - SPAN), BLOCK)
        variant = jnp.where(qs == 0, 0, jnp.where(qs == SEQ - BLOCK, 2, 1))
        for h in range(N_KV_HEADS):
            kw = kv_ref[pl.ds(ws, SPAN), LANES * h:LANES * (h + 1)]
            vw = kv_ref[pl.ds(ws, SPAN), 2 * KV_CH + LANES * h:2 * KV_CH + LANES * (h + 1)]
            qc = [q[rows, 2 * LANES * h:2 * LANES * h + LANES],
                  q[rows, 2 * LANES * h + LANES:2 * LANES * (h + 1)]]
            qst = jnp.concatenate([jnp.where(low_half, qc[0], zero_q), jnp.where(low_half, zero_q, qc[0]),
                                   jnp.where(low_half, qc[1], zero_q), jnp.where(low_half, zero_q, qc[1])], axis=0)
            sc = lax.dot_general(qst, kw, nt_dims, preferred_element_type=F32)
            probs = []
            inv_den = []
            for g in range(GQA_GROUP):
                hq = GQA_GROUP * h + g
                sg = sc[BLOCK * g:BLOCK * (g + 1), :] + bias_tab[variant * N_Q_HEADS + hq]
                sink = sink_ref[hq]
                m = jnp.maximum(jnp.max(sg, axis=-1, keepdims=True), sink)
                p = jnp.exp(sg - m)
                den = jnp.sum(p, axis=-1, keepdims=True) + jnp.exp(sink - m)
                probs.append(p.astype(BF16))
                inv_den.append(den)
            pv = jnp.dot(jnp.concatenate(probs, axis=0), vw, preferred_element_type=F32)
            og = [pv[BLOCK * g:BLOCK * (g + 1), :] / inv_den[g] for g in range(GQA_GROUP)]
            obuf[rows, 2 * LANES * h:2 * LANES * h + LANES] = jnp.where(low_half, og[0], og[1]).astype(BF16)
            obuf[rows, 2 * LANES * h + LANES:2 * LANES * (h + 1)] = jnp.where(low_half, og[2], og[3]).astype(BF16)
    y_attn = jnp.dot(obuf[...], wao_ref[...], preferred_element_type=F32)

    gates = jax.nn.sigmoid(jnp.dot(xb, wg_ref[...], preferred_element_type=F32) + bg_ref[...])
    merged = gates[:, :D_MODEL] * y_conv + gates[:, D_MODEL:] * y_attn
    out = jnp.dot(merged.astype(BF16), wo_ref[...], preferred_element_type=F32)
    o_ref[...] = _layer_norm(DN_ALPHA * xt + out, l1g_ref[...], l1b_ref[...])


def _mixer_call(sink, x, u, kv, wq, bq, wg, bg, cw, cb, clg, clb, wco, wao, wo, l1g, l1b):
    ts = MIX_TS
    ns = SEQ // ts
    const = lambda *shape: pl.BlockSpec(shape, lambda b, s, sk: (0,) * len(shape),
                                        pipeline_mode=pl.Buffered(1))
    grid_spec = pltpu.PrefetchScalarGridSpec(
        num_scalar_prefetch=1,
        grid=(BATCH, ns),
        in_specs=[pl.BlockSpec((ts, D_MODEL), lambda b, s, sk: (b * ns + s, 0)),
                  pl.BlockSpec((None, SEQ, CONV_CH), lambda b, s, sk: (b, 0, 0)),
                  pl.BlockSpec((None, SEQ, 4 * KV_CH), lambda b, s, sk: (b, 0, 0)),
                  const(D_MODEL, D_MODEL), const(1, D_MODEL),
                  const(D_MODEL, 2 * D_MODEL), const(1, 2 * D_MODEL),
                  const(N_LANE_CHUNKS, 32, LANES), const(1, CONV_CH), const(1, CONV_CH), const(1, CONV_CH),
                  const(CONV_CH, D_MODEL), const(D_MODEL, D_MODEL), const(D_MODEL, D_MODEL),
                  const(1, D_MODEL), const(1, D_MODEL)],
        out_specs=pl.BlockSpec((ts, D_MODEL), lambda b, s, sk: (b * ns + s, 0)),
        scratch_shapes=[pltpu.VMEM((N_LANE_CHUNKS, ts + 2 * HALO, LANES), F32),
                        pltpu.VMEM((ts, CONV_CH), F32),
                        pltpu.VMEM((ts, D_MODEL), BF16),
                        pltpu.VMEM((N_VARIANTS * N_Q_HEADS, BLOCK, SPAN), F32)],
    )
    return pl.pallas_call(
        _mixer_kernel,
        grid_spec=grid_spec,
        out_shape=jax.ShapeDtypeStruct((TOKENS, D_MODEL), F32),
        compiler_params=_cparams("arbitrary", "arbitrary"),
        name="mixer",
    )(sink, x, u, kv, wq, bq, wg, bg, cw, cb, clg, clb, wco, wao, wo, l1g, l1b)


ROUTER_ROWS = 32


N_PAIRS = 6
N_CLASSES = N_GROUPS * N_PAIRS
PAIR_LO = (0, 0, 0, 1, 1, 2)
PAIR_HI = (1, 2, 3, 2, 3, 3)
AUG = D_MODEL + LANES
META_ROWS = 8


def _router_kernel(x_ref, w_ref, b_ref, xaug_ref, meta_ref, cnt_ref, carry_ref):
    @pl.when(pl.program_id(0) == 0)
    def _():
        carry_ref[...] = jnp.zeros_like(carry_ref)

    x = x_ref[...]
    tm = x.shape[0]
    xh = x.astype(BF16)
    xl = (x - xh.astype(F32)).astype(BF16)
    nt_dims = (((1,), (1,)), ((), ()))
    lt = (lax.dot_general(w_ref[0], xh, nt_dims, preferred_element_type=F32)
          + lax.dot_general(w_ref[1], xh, nt_dims, preferred_element_type=F32)
          + lax.dot_general(w_ref[0], xl, nt_dims, preferred_element_type=F32))
    lt = lt + b_ref[...]
    row4 = lax.broadcasted_iota(jnp.int32, (N_GROUPS, tm), 0)
    gl = lt[0:N_GROUPS, :]
    gmax = jnp.max(gl, axis=0, keepdims=True)
    g_w = 1.0 / jnp.sum(jnp.exp(gl - gmax), axis=0, keepdims=True)
    g_idx = jnp.min(jnp.where(gl == gmax, row4, N_GROUPS), axis=0, keepdims=True)
    e_sel = jnp.zeros((EXPERTS_PER_GROUP, tm), F32)
    for g in range(N_GROUPS):
        lo = N_GROUPS + EXPERTS_PER_GROUP * g
        e_sel = e_sel + jnp.where(g_idx == g, lt[lo:lo + EXPERTS_PER_GROUP, :], 0.0)
    e1 = jnp.max(e_sel, axis=0, keepdims=True)
    i1 = jnp.min(jnp.where(e_sel == e1, row4, EXPERTS_PER_GROUP), axis=0, keepdims=True)
    rest = jnp.where(row4 == i1, -jnp.inf, e_sel)
    e2 = jnp.max(rest, axis=0, keepdims=True)
    i2 = jnp.min(jnp.where(rest == e2, row4, EXPERTS_PER_GROUP), axis=0, keepdims=True)
    t = jnp.exp(e2 - e1)
    w1 = (1.0 / (1.0 + t)) * g_w
    w2 = (t / (1.0 + t)) * g_w
    first_lower = i1 < i2
    e_lo = jnp.minimum(i1, i2)
    e_hi = jnp.maximum(i1, i2)
    w_lo = jnp.where(first_lower, w1, w2)
    w_hi = jnp.where(first_lower, w2, w1)
    pair = jnp.where(e_lo == 0, 0, jnp.where(e_lo == 1, 3, 5)) + e_hi - e_lo - 1
    cls = g_idx * N_PAIRS + pair

    row32 = lax.broadcasted_iota(jnp.int32, (ROUTER_ROWS, tm), 0)
    member = row32 == cls
    onehot = jnp.where(member, 1.0, 0.0)
    src = lax.broadcasted_iota(jnp.int32, (tm, tm), 0)
    dst = lax.broadcasted_iota(jnp.int32, (tm, tm), 1)
    upper = jnp.where(src <= dst, 1.0, 0.0).astype(BF16)
    prefix = jnp.dot(onehot.astype(BF16), upper, preferred_element_type=F32)
    carry = carry_ref[:, 0:1]
    rank = jnp.sum(jnp.where(member, prefix - 1.0 + carry, 0.0), axis=0, keepdims=True)
    carry_ref[...] = carry_ref[...] + jnp.sum(onehot, axis=1, keepdims=True)
    cnt_ref[...] = carry_ref[...]

    meta_ref[...] = jnp.concatenate(
        [cls.astype(F32), rank, jnp.zeros((META_ROWS - 2, tm), F32)], axis=0)
    wrows = jnp.concatenate([w_lo, w_hi, jnp.zeros((LANES - 2, tm), F32)], axis=0)
    xaug_ref[:, :D_MODEL] = x
    xaug_ref[:, D_MODEL:] = wrows.T


def _router_call(x, w, b, tm=512):
    return pl.pallas_call(
        _router_kernel,
        grid=(TOKENS // tm,),
        in_specs=[pl.BlockSpec((tm, D_MODEL), lambda i: (i, 0)),
                  pl.BlockSpec((2, ROUTER_ROWS, D_MODEL), lambda i: (0, 0, 0)),
                  pl.BlockSpec((ROUTER_ROWS, tm), lambda i: (0, 0))],
        out_specs=[pl.BlockSpec((tm, AUG), lambda i: (i, 0)),
                   pl.BlockSpec((META_ROWS, tm), lambda i: (0, i)),
                   pl.BlockSpec((ROUTER_ROWS, LANES), lambda i: (0, 0))],
        out_shape=[jax.ShapeDtypeStruct((TOKENS, AUG), F32),
                   jax.ShapeDtypeStruct((META_ROWS, TOKENS), F32),
                   jax.ShapeDtypeStruct((ROUTER_ROWS, LANES), F32)],
        scratch_shapes=[pltpu.VMEM((ROUTER_ROWS, LANES), F32)],
        compiler_params=_cparams("arbitrary"),
        name="router",
    )(x, w, b)


MOE_TM = 256
MOE_TILES = TOKENS // MOE_TM
MOE_ITEMS = MOE_TILES + N_CLASSES - 1
FLAG_VALID, FLAG_FIRST, FLAG_LAST = 1, 2, 4


def _dispatch_tables(meta, cnt):
    cls = meta[0].astype(jnp.int32)
    rank = meta[1].astype(jnp.int32)
    counts = cnt[:N_CLASSES, 0].astype(jnp.int32)
    starts = jnp.concatenate([jnp.zeros((1,), jnp.int32), jnp.cumsum(counts)])
    pos = starts[cls] + rank
    tok_sorted = jnp.zeros((TOKENS,), jnp.int32).at[pos].set(jnp.arange(TOKENS, dtype=jnp.int32))
    cuts = jnp.sort(jnp.concatenate([jnp.arange(MOE_TILES, dtype=jnp.int32) * MOE_TM, starts[1:N_CLASSES]]))
    nxt = jnp.concatenate([cuts[1:], jnp.full((1,), TOKENS, jnp.int32)])
    tile = jnp.minimum(cuts // MOE_TM, MOE_TILES - 1)
    lo = cuts - tile * MOE_TM
    hi = nxt - tile * MOE_TM
    valid = nxt > cuts
    c = jnp.minimum(jnp.sum(starts[1:][None, :] <= cuts[:, None], axis=1), N_CLASSES - 1)
    grp, pair = c // N_PAIRS, c % N_PAIRS
    e_lo = grp * EXPERTS_PER_GROUP + jnp.asarray(PAIR_LO, jnp.int32)[pair]
    e_hi = grp * EXPERTS_PER_GROUP + jnp.asarray(PAIR_HI, jnp.int32)[pair]
    flags = (valid * FLAG_VALID + (valid & (lo == 0)) * FLAG_FIRST
             + (valid & (hi == MOE_TM)) * FLAG_LAST).astype(jnp.int32)
    i32 = lambda v: v.astype(jnp.int32)
    return tok_sorted.reshape(MOE_TILES, 1, MOE_TM), i32(tile), i32(e_lo), i32(e_hi), i32(lo), i32(hi), flags


def _moe_kernel(tile_ref, elo_ref, ehi_ref, lo_ref, hi_ref, flag_ref,
                tok_cur_ref, tok_next_ref, xaug_hbm,
                w1a_ref, w3a_ref, w2a_ref, w1b_ref, w3b_ref, w2b_ref,
                y_hbm,
                xbuf, obuf, acc_ref, gsem, ssem):
    w = pl.program_id(0)
    flags = flag_ref[w]
    tile = tile_ref[w]
    slot = tile % 2

    def row_gather(tok_ref, dst_slot):
        def body(r, carry):
            pltpu.make_async_copy(xaug_hbm.at[pl.ds(tok_ref[0, r], 1)],
                                  xbuf.at[dst_slot, pl.ds(r, 1)], gsem.at[dst_slot]).start()
            return carry
        lax.fori_loop(0, MOE_TM, body, 0, unroll=8)

    def gather_wait(dst_slot):
        pltpu.make_async_copy(xaug_hbm.at[pl.ds(0, MOE_TM)], xbuf.at[dst_slot], gsem.at[dst_slot]).wait()

    def scatter_wait():
        pltpu.make_async_copy(obuf, y_hbm.at[pl.ds(0, MOE_TM)], ssem.at[0]).wait()

    @pl.when((flags & FLAG_FIRST) != 0)
    def _():
        @pl.when(tile == 0)
        def _():
            row_gather(tok_cur_ref, slot)

        gather_wait(slot)

        @pl.when(tile + 1 < MOE_TILES)
        def _():
            row_gather(tok_next_ref, 1 - slot)

        acc_ref[...] = jnp.zeros_like(acc_ref)

    @pl.when((flags & FLAG_VALID) != 0)
    def _():
        xa = xbuf[slot]
        xb = xa[:, :D_MODEL].astype(BF16)
        row = lax.broadcasted_iota(jnp.int32, (MOE_TM, 1), 0)
        mine = (row >= lo_ref[w]) & (row < hi_ref[w])
        wa = jnp.where(mine, xa[:, D_MODEL:D_MODEL + 1], 0.0)
        wb = jnp.where(mine, xa[:, D_MODEL + 1:D_MODEL + 2], 0.0)
        a1 = jnp.dot(xb, w1a_ref[...], preferred_element_type=F32)
        a3 = jnp.dot(xb, w3a_ref[...], preferred_element_type=F32)
        ha = ((a1 * jax.nn.sigmoid(a1)) * a3 * wa).astype(BF16)
        b1 = jnp.dot(xb, w1b_ref[...], preferred_element_type=F32)
        b3 = jnp.dot(xb, w3b_ref[...], preferred_element_type=F32)
        hb = ((b1 * jax.nn.sigmoid(b1)) * b3 * wb).astype(BF16)
        acc_ref[...] += (jnp.dot(ha, w2a_ref[...], preferred_element_type=F32)
                         + jnp.dot(hb, w2b_ref[...], preferred_element_type=F32))

    @pl.when((flags & FLAG_LAST) != 0)
    def _():
        @pl.when(tile > 0)
        def _():
            scatter_wait()

        obuf[...] = acc_ref[...]

        def body(r, carry):
            pltpu.make_async_copy(obuf.at[pl.ds(r, 1)], y_hbm.at[pl.ds(tok_cur_ref[0, r], 1)],
                                  ssem.at[0]).start()
            return carry
        lax.fori_loop(0, MOE_TM, body, 0, unroll=8)

        @pl.when(tile == MOE_TILES - 1)
        def _():
            scatter_wait()


def _moe_call(tables, xaug, w1, w3, w2):
    tok_sorted, tile, e_lo, e_hi, lo, hi, flags = tables
    smem_tok = lambda fn: pl.BlockSpec((None, 1, MOE_TM), fn, memory_space=pltpu.SMEM)
    wspec = lambda shape, which: pl.BlockSpec(
        (None,) + shape, lambda w, t, el, eh, lo, hi, fl: ((el, eh)[which][w], 0, 0))
    grid_spec = pltpu.PrefetchScalarGridSpec(
        num_scalar_prefetch=6,
        grid=(MOE_ITEMS,),
        in_specs=[smem_tok(lambda w, t, *_: (t[w], 0, 0)),
                  smem_tok(lambda w, t, *_: (jnp.minimum(t[w] + 1, MOE_TILES - 1), 0, 0)),
                  pl.BlockSpec(memory_space=pl.ANY),
                  wspec((D_MODEL, D_EXPERT), 0), wspec((D_MODEL, D_EXPERT), 0), wspec((D_EXPERT, D_MODEL), 0),
                  wspec((D_MODEL, D_EXPERT), 1), wspec((D_MODEL, D_EXPERT), 1), wspec((D_EXPERT, D_MODEL), 1)],
        out_specs=pl.BlockSpec(memory_space=pl.ANY),
        scratch_shapes=[pltpu.VMEM((2, MOE_TM, AUG), F32),
                        pltpu.VMEM((MOE_TM, D_MODEL), F32),
                        pltpu.VMEM((MOE_TM, D_MODEL), F32),
                        pltpu.SemaphoreType.DMA((2,)),
                        pltpu.SemaphoreType.DMA((1,))],
    )
    return pl.pallas_call(
        _moe_kernel,
        grid_spec=grid_spec,
        out_shape=jax.ShapeDtypeStruct((TOKENS, D_MODEL), F32),
        compiler_params=_cparams("arbitrary"),
        name="moe",
    )(tile, e_lo, e_hi, lo, hi, flags, tok_sorted, tok_sorted, xaug, w1, w3, w2, w1, w3, w2)


def _ple_ln_kernel(x_ref, f_ref, p_ref, wpg_ref, bpg_ref, wp_ref, g_ref, b_ref, o_ref):
    x = x_ref[...]
    gate = jax.nn.sigmoid(jnp.dot(x.astype(BF16), wpg_ref[...], preferred_element_type=F32) + bpg_ref[...])
    pe = jnp.dot(p_ref[...].astype(BF16), wp_ref[...], preferred_element_type=F32)
    o_ref[...] = _layer_norm(DN_ALPHA * x + f_ref[...] + gate * pe, g_ref[...], b_ref[...])


def _ple_ln_call(x, f, p, wpg, bpg, wp, g, b, tm=512):
    row = lambda n: pl.BlockSpec((tm, n), lambda i: (i, 0))
    const = lambda *shape: pl.BlockSpec(shape, lambda i: (0,) * len(shape))
    return pl.pallas_call(
        _ple_ln_kernel,
        grid=(TOKENS // tm,),
        in_specs=[row(D_MODEL), row(D_MODEL), row(D_PLE),
                  const(D_MODEL, D_MODEL), const(1, D_MODEL), const(D_PLE, D_MODEL),
                  const(1, D_MODEL), const(1, D_MODEL)],
        out_specs=row(D_MODEL),
        out_shape=jax.ShapeDtypeStruct((TOKENS, D_MODEL), F32),
        compiler_params=_cparams("arbitrary"),
        name="ple_ln",
    )(x, f, p, wpg, bpg, wp, g, b)


def _dup_heads(w):
    lead = w.shape[:-1]
    w4 = w.reshape(lead + (N_KV_HEADS, 1, HEAD_DIM))
    return jnp.broadcast_to(w4, lead + (N_KV_HEADS, 2, HEAD_DIM)).reshape(lead + (2 * KV_CH,))


def _row(v):
    return v.reshape(1, -1)


def kernel(x, p, ln_emb_g, ln_emb_b, w_in, b_in, conv_w, conv_b, conv_ln_g, conv_ln_b, w_conv_out, w_attn_out, attn_sink, w_out, ln1_g, ln1_b, w_router_group, b_router_group, w_router_expert, b_router_expert, w1, w3, w2, w_p, w_pg, b_pg, ln2_g, ln2_b):
    c0, c1, c2, c3, c4 = (2 * CONV_CH, 2 * CONV_CH + D_MODEL, 2 * CONV_CH + D_MODEL + KV_CH,
                          2 * CONV_CH + D_MODEL + 2 * KV_CH, 2 * CONV_CH + 2 * D_MODEL + 2 * KV_CH)
    q_scale = HEAD_DIM ** -0.5
    router_tm = 512

    xs = _ln_call(x.reshape(TOKENS, D_MODEL), _row(ln_emb_g), _row(ln_emb_b))
    for i in range(DEPTH):
        w, b = w_in[i], b_in[i]
        wa = w[:, :CONV_CH].astype(BF16)
        wgl = w[:, CONV_CH:c0].astype(BF16)
        wq = (w[:, c0:c1] * q_scale).astype(BF16)
        bq = _row(b[c0:c1] * q_scale)
        wkv = jnp.concatenate([_dup_heads(w[:, c1:c2]), _dup_heads(w[:, c2:c3])], axis=1).astype(BF16)
        bkv = _row(jnp.concatenate([_dup_heads(b[c1:c2]), _dup_heads(b[c2:c3])]))
        wgate = w[:, c3:].astype(BF16)
        bgate = _row(b[c3:])
        cw = jnp.pad(conv_w[i], ((0, 1), (0, 0))).reshape(32, N_LANE_CHUNKS, LANES).transpose(1, 0, 2)

        u = _glu_call(xs, wa, wgl, _row(b[:CONV_CH]), _row(b[CONV_CH:c0]))
        kv = _kv_call(xs, wkv, bkv)
        x1 = _mixer_call(attn_sink[i].astype(F32), xs,
                         u.reshape(BATCH, SEQ, CONV_CH), kv.reshape(BATCH, SEQ, 4 * KV_CH),
                         wq, bq, wgate, bgate, cw, _row(conv_b[i]), _row(conv_ln_g[i]), _row(conv_ln_b[i]),
                         w_conv_out[i].astype(BF16), w_attn_out[i].astype(BF16), w_out[i].astype(BF16),
                         _row(ln1_g[i]), _row(ln1_b[i]))

        wr = jnp.concatenate([w_router_group[i], w_router_expert[i]], axis=1).T
        wr = jnp.pad(wr, ((0, ROUTER_ROWS - wr.shape[0]), (0, 0)))
        wr_hi = wr.astype(BF16)
        wr_lo = (wr - wr_hi.astype(F32)).astype(BF16)
        br = jnp.pad(jnp.concatenate([b_router_group[i], b_router_expert[i]]), (0, ROUTER_ROWS - 20))
        br = jnp.broadcast_to(br[:, None], (ROUTER_ROWS, router_tm))
        xaug, meta, cnt = _router_call(x1, jnp.stack([wr_hi, wr_lo]), br, tm=router_tm)

        ffn = _moe_call(_dispatch_tables(meta, cnt), xaug,
                        w1[i].reshape(N_EXPERTS, D_MODEL, D_EXPERT).astype(BF16),
                        w3[i].reshape(N_EXPERTS, D_MODEL, D_EXPERT).astype(BF16),
                        w2[i].reshape(N_EXPERTS, D_EXPERT, D_MODEL).astype(BF16))
        xs = _ple_ln_call(x1, ffn, p[i].reshape(TOKENS, D_PLE), w_pg[i].astype(BF16), _row(b_pg[i]),
                          w_p[i].astype(BF16), _row(ln2_g[i]), _row(ln2_b[i]))
    return xs.reshape(BATCH, SEQ, D_MODEL)
```

```python
import functools

import jax
import jax.numpy as jnp
from jax import lax
from jax.experimental import pallas as pl
from jax.experimental.pallas import tpu as pltpu

D_MODEL = 1024
BATCH = 16
SEQ = 2048
DEPTH = 2
TOKENS = BATCH * SEQ
CONV_CH = D_MODEL
CONV_WIDTH = 31
CONV_PAD = CONV_WIDTH // 2
HEAD_DIM = 64
N_Q_HEADS = 16
N_KV_HEADS = 4
GQA_GROUP = 4
KV_CH = N_KV_HEADS * HEAD_DIM
WINDOW = 128
BLOCK = 128
SPAN = BLOCK + 2 * WINDOW
N_GROUPS = 4
EXPERTS_PER_GROUP = 4
N_EXPERTS = N_GROUPS * EXPERTS_PER_GROUP
D_EXPERT = 512
D_PLE = 256
DN_ALPHA = (2 * DEPTH) ** 0.25
LN_EPS = 1e-5
NEG_INF = -1e30

LANES = 128
N_LANE_CHUNKS = D_MODEL // LANES
VMEM_LIMIT = 56 * 1024 * 1024

F32 = jnp.float32
BF16 = jnp.bfloat16


def _layer_norm(v, g, b):
    mu = jnp.mean(v, axis=-1, keepdims=True)
    d = v - mu
    var = jnp.mean(d * d, axis=-1, keepdims=True)
    return d * lax.rsqrt(var + LN_EPS) * g + b


def _cparams(*sem):
    return pltpu.CompilerParams(dimension_semantics=sem, vmem_limit_bytes=VMEM_LIMIT)


def _ln_kernel(x_ref, g_ref, b_ref, o_ref):
    o_ref[...] = _layer_norm(x_ref[...], g_ref[...], b_ref[...])


def _ln_call(x, g, b, tm=512):
    return pl.pallas_call(
        _ln_kernel,
        grid=(TOKENS // tm,),
        in_specs=[pl.BlockSpec((tm, D_MODEL), lambda i: (i, 0)),
                  pl.BlockSpec((1, D_MODEL), lambda i: (0, 0)),
                  pl.BlockSpec((1, D_MODEL), lambda i: (0, 0))],
        out_specs=pl.BlockSpec((tm, D_MODEL), lambda i: (i, 0)),
        out_shape=jax.ShapeDtypeStruct((TOKENS, D_MODEL), F32),
        compiler_params=_cparams("arbitrary"),
        name="ln_emb",
    )(x, g, b)


GLU_BN = 256
CONV_RB = 64
HALO = 16


def _glu_conv_kernel(x_ref, wa_ref, wg_ref, ba_ref, bg_ref, cw_ref, cb_ref, y_ref, ubuf):
    xb = x_ref[...].astype(BF16)
    a = jnp.dot(xb, wa_ref[...], preferred_element_type=F32) + ba_ref[...]
    g = jnp.dot(xb, wg_ref[...], preferred_element_type=F32) + bg_ref[...]
    u = a * jax.nn.sigmoid(g)
    n_chunks = GLU_BN // LANES
    for k in range(n_chunks):
        ubuf[k, 0:HALO, :] = jnp.zeros((HALO, LANES), F32)
        ubuf[k, HALO:HALO + SEQ, :] = u[:, LANES * k:LANES * (k + 1)]
        ubuf[k, HALO + SEQ:2 * HALO + SEQ, :] = jnp.zeros((HALO, LANES), F32)
    base = HALO - CONV_PAD
    for k in range(n_chunks):
        cs = slice(LANES * k, LANES * (k + 1))
        wk = cw_ref[k]
        for rb in range(SEQ // CONV_RB):
            r0 = rb * CONV_RB
            acc = jnp.broadcast_to(cb_ref[:, cs], (CONV_RB, LANES))
            for j in range(CONV_WIDTH):
                acc = acc + ubuf[k, r0 + base + j:r0 + base + j + CONV_RB, :] * wk[j:j + 1, :]
            y_ref[r0:r0 + CONV_RB, cs] = acc.astype(BF16)


def _glu_conv_call(x, wa, wg, ba, bg, cw, cb):
    bn = GLU_BN
    return pl.pallas_call(
        _glu_conv_kernel,
        grid=(BATCH, CONV_CH // bn),
        in_specs=[pl.BlockSpec((SEQ, D_MODEL), lambda b, j: (b, 0)),
                  pl.BlockSpec((D_MODEL, bn), lambda b, j: (0, j)),
                  pl.BlockSpec((D_MODEL, bn), lambda b, j: (0, j)),
                  pl.BlockSpec((1, bn), lambda b, j: (0, j)),
                  pl.BlockSpec((1, bn), lambda b, j: (0, j)),
                  pl.BlockSpec((bn // LANES, 32, LANES), lambda b, j: (j, 0, 0)),
                  pl.BlockSpec((1, bn), lambda b, j: (0, j))],
        out_specs=pl.BlockSpec((SEQ, bn), lambda b, j: (b, j)),
        out_shape=jax.ShapeDtypeStruct((TOKENS, CONV_CH), BF16),
        scratch_shapes=[pltpu.VMEM((bn // LANES, SEQ + 2 * HALO, LANES), F32)],
        compiler_params=_cparams("arbitrary", "arbitrary"),
        name="glu_conv",
    )(x, wa, wg, ba, bg, cw, cb)


def _proj_kernel(x_ref, w_ref, b_ref, o_ref):
    acc = jnp.dot(x_ref[...].astype(BF16), w_ref[...], preferred_element_type=F32)
    o_ref[...] = (acc + b_ref[...]).astype(BF16)


def _kv_call(x, w, b, tm=1024, bn=512):
    n = w.shape[1]
    return pl.pallas_call(
        _proj_kernel,
        grid=(TOKENS // tm, n // bn),
        in_specs=[pl.BlockSpec((tm, D_MODEL), lambda i, j: (i, 0)),
                  pl.BlockSpec((D_MODEL, bn), lambda i, j: (0, j)),
                  pl.BlockSpec((1, bn), lambda i, j: (0, j))],
        out_specs=pl.BlockSpec((tm, bn), lambda i, j: (i, j)),
        out_shape=jax.ShapeDtypeStruct((TOKENS, n), BF16),
        compiler_params=_cparams("arbitrary", "arbitrary"),
        name="kv_proj",
    )(x, w, b)


MIX_TS = 256
N_VARIANTS = 3


def _alibi_slope(h):
    return 2.0 ** (-8.0 * (h + 1) / N_Q_HEADS)


def _mixer_kernel(sink_ref,
                  x_ref, yc_ref, kv_ref, wq_ref, bq_ref, wg_ref, bg_ref,
                  clg_ref, clb_ref,
                  wco_ref, wao_ref, wo_ref, l1g_ref, l1b_ref,
                  o_ref,
                  obuf, bias_tab):
    b = pl.program_id(0)
    s = pl.program_id(1)
    ts = MIX_TS
    t0 = pl.multiple_of(s * ts, ts)

    @pl.when((b == 0) & (s == 0))
    def _():
        r = lax.broadcasted_iota(jnp.int32, (BLOCK, SPAN), 0)
        kk = lax.broadcasted_iota(jnp.int32, (BLOCK, SPAN), 1)
        for v in range(N_VARIANTS):
            dist = jnp.abs(kk - r - v * WINDOW)
            inside = dist <= WINDOW
            distf = dist.astype(F32)
            for h in range(N_Q_HEADS):
                bias_tab[v * N_Q_HEADS + h] = jnp.where(inside, -_alibi_slope(h) * distf, NEG_INF)

    yn = _layer_norm(yc_ref[...].astype(F32), clg_ref[...], clb_ref[...])
    act = (yn * jax.nn.sigmoid(yn)).astype(BF16)
    y_conv = jnp.dot(act, wco_ref[...], preferred_element_type=F32)

    xt = x_ref[...]
    xb = xt.astype(BF16)
    q = (jnp.dot(xb, wq_ref[...], preferred_element_type=F32) + bq_ref[...]).astype(BF16)
    low_half = lax.broadcasted_iota(jnp.int32, (BLOCK, LANES), 1) < HEAD_DIM
    zero_q = jnp.zeros((BLOCK, LANES), BF16)
    nt_dims = (((1,), (1,)), ((), ()))
    for qi in range(ts // BLOCK):
        rows = slice(BLOCK * qi, BLOCK * (qi + 1))
        qs = t0 + BLOCK * qi
        ws = pl.multiple_of(jnp.clip(qs - WINDOW, 0, SEQ - SPAN), BLOCK)
        variant = jnp.where(qs == 0, 0, jnp.where(qs == SEQ - BLOCK, 2, 1))
        for h in range(N_KV_HEADS):
            kw = kv_ref[pl.ds(ws, SPAN), LANES * h:LANES * (h + 1)]
            vw = kv_ref[pl.ds(ws, SPAN), 2 * KV_CH + LANES * h:2 * KV_CH + LANES * (h + 1)]
            qc = [q[rows, 2 * LANES * h:2 * LANES * h + LANES],
                  q[rows, 2 * LANES * h + LANES:2 * LANES * (h + 1)]]
            qst = jnp.concatenate([jnp.where(low_half, qc[0], zero_q), jnp.where(low_half, zero_q, qc[0]),
                                   jnp.where(low_half, qc[1], zero_q), jnp.where(low_half, zero_q, qc[1])], axis=0)
            sc = lax.dot_general(qst, kw, nt_dims, preferred_element_type=F32)
            probs = []
            inv_den = []
            for g in range(GQA_GROUP):
                hq = GQA_GROUP * h + g
                sg = sc[BLOCK * g:BLOCK * (g + 1), :] + bias_tab[variant * N_Q_HEADS + hq]
                sink = sink_ref[hq]
                m = jnp.maximum(jnp.max(sg, axis=-1, keepdims=True), sink)
                p = jnp.exp(sg - m)
                den = jnp.sum(p, axis=-1, keepdims=True) + jnp.exp(sink - m)
                probs.append(p.astype(BF16))
                inv_den.append(den)
            pv = jnp.dot(jnp.concatenate(probs, axis=0), vw, preferred_element_type=F32)
            og = [pv[BLOCK * g:BLOCK * (g + 1), :] / inv_den[g] for g in range(GQA_GROUP)]
            obuf[rows, 2 * LANES * h:2 * LANES * h + LANES] = jnp.where(low_half, og[0], og[1]).astype(BF16)
            obuf[rows, 2 * LANES * h + LANES:2 * LANES * (h + 1)] = jnp.where(low_half, og[2], og[3]).astype(BF16)
    y_attn = jnp.dot(obuf[...], wao_ref[...], preferred_element_type=F32)

    gates = jax.nn.sigmoid(jnp.dot(xb, wg_ref[...], preferred_element_type=F32) + bg_ref[...])
    merged = gates[:, :D_MODEL] * y_conv + gates[:, D_MODEL:] * y_attn
    out = jnp.dot(merged.astype(BF16), wo_ref[...], preferred_element_type=F32)
    o_ref[...] = _layer_norm(DN_ALPHA * xt + out, l1g_ref[...], l1b_ref[...])


def _mixer_call(sink, x, yc, kv, wq, bq, wg, bg, clg, clb, wco, wao, wo, l1g, l1b):
    ts = MIX_TS
    ns = SEQ // ts
    const = lambda *shape: pl.BlockSpec(shape, lambda b, s, sk: (0,) * len(shape),
                                        pipeline_mode=pl.Buffered(1))
    grid_spec = pltpu.PrefetchScalarGridSpec(
        num_scalar_prefetch=1,
        grid=(BATCH, ns),
        in_specs=[pl.BlockSpec((ts, D_MODEL), lambda b, s, sk: (b * ns + s, 0)),
                  pl.BlockSpec((ts, CONV_CH), lambda b, s, sk: (b * ns + s, 0)),
                  pl.BlockSpec((None, SEQ, 4 * KV_CH), lambda b, s, sk: (b, 0, 0)),
                  const(D_MODEL, D_MODEL), const(1, D_MODEL),
                  const(D_MODEL, 2 * D_MODEL), const(1, 2 * D_MODEL),
                  const(1, CONV_CH), const(1, CONV_CH),
                  const(CONV_CH, D_MODEL), const(D_MODEL, D_MODEL), const(D_MODEL, D_MODEL),
                  const(1, D_MODEL), const(1, D_MODEL)],
        out_specs=pl.BlockSpec((ts, D_MODEL), lambda b, s, sk: (b * ns + s, 0)),
        scratch_shapes=[pltpu.VMEM((ts, D_MODEL), BF16),
                        pltpu.VMEM((N_VARIANTS * N_Q_HEADS, BLOCK, SPAN), F32)],
    )
    return pl.pallas_call(
        _mixer_kernel,
        grid_spec=grid_spec,
        out_shape=jax.ShapeDtypeStruct((TOKENS, D_MODEL), F32),
        compiler_params=_cparams("arbitrary", "arbitrary"),
        name="mixer",
    )(sink, x, yc, kv, wq, bq, wg, bg, clg, clb, wco, wao, wo, l1g, l1b)


ROUTER_ROWS = 32


N_PAIRS = 6
N_CLASSES = N_GROUPS * N_PAIRS
PAIR_LO = (0, 0, 0, 1, 1, 2)
PAIR_HI = (1, 2, 3, 2, 3, 3)
META_ROWS = 8


def _router_kernel(x_ref, w_ref, b_ref, meta_ref, cnt_ref, carry_ref):
    @pl.when(pl.program_id(0) == 0)
    def _():
        carry_ref[...] = jnp.zeros_like(carry_ref)

    x = x_ref[...]
    tm = x.shape[0]
    xh = x.astype(BF16)
    xl = (x - xh.astype(F32)).astype(BF16)
    nt_dims = (((1,), (1,)), ((), ()))
    lt = (lax.dot_general(w_ref[0], xh, nt_dims, preferred_element_type=F32)
          + lax.dot_general(w_ref[1], xh, nt_dims, preferred_element_type=F32)
          + lax.dot_general(w_ref[0], xl, nt_dims, preferred_element_type=F32))
    lt = lt + b_ref[...]
    row4 = lax.broadcasted_iota(jnp.int32, (N_GROUPS, tm), 0)
    gl = lt[0:N_GROUPS, :]
    gmax = jnp.max(gl, axis=0, keepdims=True)
    g_w = 1.0 / jnp.sum(jnp.exp(gl - gmax), axis=0, keepdims=True)
    g_idx = jnp.min(jnp.where(gl == gmax, row4, N_GROUPS), axis=0, keepdims=True)
    e_sel = jnp.zeros((EXPERTS_PER_GROUP, tm), F32)
    for g in range(N_GROUPS):
        lo = N_GROUPS + EXPERTS_PER_GROUP * g
        e_sel = e_sel + jnp.where(g_idx == g, lt[lo:lo + EXPERTS_PER_GROUP, :], 0.0)
    e1 = jnp.max(e_sel, axis=0, keepdims=True)
    i1 = jnp.min(jnp.where(e_sel == e1, row4, EXPERTS_PER_GROUP), axis=0, keepdims=True)
    rest = jnp.where(row4 == i1, -jnp.inf, e_sel)
    e2 = jnp.max(rest, axis=0, keepdims=True)
    i2 = jnp.min(jnp.where(rest == e2, row4, EXPERTS_PER_GROUP), axis=0, keepdims=True)
    t = jnp.exp(e2 - e1)
    w1 = (1.0 / (1.0 + t)) * g_w
    w2 = (t / (1.0 + t)) * g_w
    first_lower = i1 < i2
    e_lo = jnp.minimum(i1, i2)
    e_hi = jnp.maximum(i1, i2)
    w_lo = jnp.where(first_lower, w1, w2)
    w_hi = jnp.where(first_lower, w2, w1)
    pair = jnp.where(e_lo == 0, 0, jnp.where(e_lo == 1, 3, 5)) + e_hi - e_lo - 1
    cls = g_idx * N_PAIRS + pair

    row32 = lax.broadcasted_iota(jnp.int32, (ROUTER_ROWS, tm), 0)
    member = row32 == cls
    onehot = jnp.where(member, 1.0, 0.0)
    src = lax.broadcasted_iota(jnp.int32, (tm, tm), 0)
    dst = lax.broadcasted_iota(jnp.int32, (tm, tm), 1)
    upper = jnp.where(src <= dst, 1.0, 0.0).astype(BF16)
    prefix = jnp.dot(onehot.astype(BF16), upper, preferred_element_type=F32)
    carry = carry_ref[:, 0:1]
    rank = jnp.sum(jnp.where(member, prefix - 1.0 + carry, 0.0), axis=0, keepdims=True)
    carry_ref[...] = carry_ref[...] + jnp.sum(onehot, axis=1, keepdims=True)
    cnt_ref[...] = carry_ref[...]

    meta_ref[...] = jnp.concatenate(
        [cls.astype(F32), rank, w_lo, w_hi, jnp.zeros((META_ROWS - 4, tm), F32)], axis=0)


def _router_call(x, w, b, tm=512):
    return pl.pallas_call(
        _router_kernel,
        grid=(TOKENS // tm,),
        in_specs=[pl.BlockSpec((tm, D_MODEL), lambda i: (i, 0)),
                  pl.BlockSpec((2, ROUTER_ROWS, D_MODEL), lambda i: (0, 0, 0)),
                  pl.BlockSpec((ROUTER_ROWS, tm), lambda i: (0, 0))],
        out_specs=[pl.BlockSpec((META_ROWS, tm), lambda i: (0, i)),
                   pl.BlockSpec((ROUTER_ROWS, LANES), lambda i: (0, 0))],
        out_shape=[jax.ShapeDtypeStruct((META_ROWS, TOKENS), F32),
                   jax.ShapeDtypeStruct((ROUTER_ROWS, LANES), F32)],
        scratch_shapes=[pltpu.VMEM((ROUTER_ROWS, LANES), F32)],
        compiler_params=_cparams("arbitrary"),
        name="router",
    )(x, w, b)


MOE_TM = 256
TOK_ROWS = 8
MOE_ITEMS = TOKENS // MOE_TM + N_CLASSES
SORTED_TILES = MOE_ITEMS + 1
SORTED_TOKENS = SORTED_TILES * MOE_TM
MAX_IDLE_TILES = SORTED_TILES - TOKENS // MOE_TM
DISPATCH_TM = 512
HI_MASK = 0xFFFF0000


def _dispatch_tables(meta, cnt):
    cls = meta[0].astype(jnp.int32)
    rank = meta[1].astype(jnp.int32)
    counts = cnt[:N_CLASSES, 0].astype(jnp.int32)
    tiles_per_class = (counts + MOE_TM - 1) // MOE_TM
    tile_end = jnp.cumsum(tiles_per_class)
    starts = (tile_end - tiles_per_class) * MOE_TM
    pos = starts[cls] + rank
    fill_from = starts + counts
    n_tiles = tile_end[-1]
    j = jnp.arange(MOE_ITEMS, dtype=jnp.int32)
    active = j < n_tiles
    jj = jnp.minimum(j, n_tiles - 1)
    c = jnp.minimum(jnp.sum(tile_end[None, :] <= jj[:, None], axis=1), N_CLASSES - 1)
    grp, pair = c // N_PAIRS, c % N_PAIRS
    e_lo = grp * EXPERTS_PER_GROUP + jnp.asarray(PAIR_LO, jnp.int32)[pair]
    e_hi = grp * EXPERTS_PER_GROUP + jnp.asarray(PAIR_HI, jnp.int32)[pair]
    i32 = lambda v: v.astype(jnp.int32)
    fill = jnp.concatenate([i32(fill_from), i32(n_tiles).reshape(1)])
    return (i32(pos).reshape(TOKENS // DISPATCH_TM, 1, DISPATCH_TM), fill,
            i32(jj), i32(e_lo), i32(e_hi), i32(active))


def _dispatch_kernel(fill_ref, pos_ref, x_ref, meta_ref, xs_hbm, stage, zeros, sem, zsem):
    i = pl.program_id(0)
    n = pl.num_programs(0)
    tm = DISPATCH_TM
    slot = i % 2

    def slab_wait(s):
        pltpu.make_async_copy(stage.at[s], xs_hbm.at[pl.ds(0, tm * TOK_ROWS)], sem.at[s]).wait()

    @pl.when(i == 0)
    def _():
        zeros[...] = jnp.zeros_like(zeros)
        for c in range(N_CLASSES):
            dst = pl.multiple_of(fill_ref[c] * TOK_ROWS, TOK_ROWS)
            pltpu.make_async_copy(zeros, xs_hbm.at[pl.ds(dst, MOE_TM * TOK_ROWS)], zsem.at[0]).start()
        for c in range(N_CLASSES):
            pltpu.make_async_copy(zeros, xs_hbm.at[pl.ds(0, MOE_TM * TOK_ROWS)], zsem.at[0]).wait()
        for k in range(MAX_IDLE_TILES):
            t = fill_ref[N_CLASSES] + k

            @pl.when(t < SORTED_TILES)
            def _():
                dst = pl.multiple_of(t * (MOE_TM * TOK_ROWS), MOE_TM * TOK_ROWS)
                tail = pltpu.make_async_copy(zeros, xs_hbm.at[pl.ds(dst, MOE_TM * TOK_ROWS)], zsem.at[0])
                tail.start()
                tail.wait()

    @pl.when(i >= 2)
    def _():
        slab_wait(slot)

    x = x_ref[...]
    for g in range(D_MODEL // (2 * LANES)):
        lo = x[:, 2 * LANES * g:2 * LANES * g + LANES].astype(BF16).astype(F32)
        hi = x[:, 2 * LANES * g + LANES:2 * LANES * (g + 1)].astype(BF16).astype(F32)
        word = (lax.shift_right_logical(lax.bitcast_convert_type(lo, jnp.uint32), jnp.uint32(16))
                | (lax.bitcast_convert_type(hi, jnp.uint32) & jnp.uint32(HI_MASK)))
        stage[slot, pl.ds(g, tm, stride=TOK_ROWS), :] = word
    wrows = jnp.concatenate([meta_ref[2:4, :], jnp.zeros((LANES - 2, tm), F32)], axis=0)
    stage[slot, pl.ds(4, tm, stride=TOK_ROWS), :] = lax.bitcast_convert_type(wrows.T, jnp.uint32)
    for g in range(5, TOK_ROWS):
        stage[slot, pl.ds(g, tm, stride=TOK_ROWS), :] = jnp.zeros((tm, LANES), jnp.uint32)

    def body(r, carry):
        dst = pl.multiple_of(pos_ref[0, r] * TOK_ROWS, TOK_ROWS)
        src = pl.multiple_of(r * TOK_ROWS, TOK_ROWS)
        pltpu.make_async_copy(stage.at[slot, pl.ds(src, TOK_ROWS)], xs_hbm.at[pl.ds(dst, TOK_ROWS)],
                              sem.at[slot]).start()
        return carry
    lax.fori_loop(0, tm, body, 0, unroll=8)

    @pl.when(i == n - 1)
    def _():
        slab_wait(1 - slot)
        slab_wait(slot)


def _dispatch_call(tables, x, meta):
    pos, fill_from = tables[0], tables[1]
    tm = DISPATCH_TM
    grid_spec = pltpu.PrefetchScalarGridSpec(
        num_scalar_prefetch=1,
        grid=(TOKENS // tm,),
        in_specs=[pl.BlockSpec((None, 1, tm), lambda i, f: (i, 0, 0), memory_space=pltpu.SMEM),
                  pl.BlockSpec((tm, D_MODEL), lambda i, f: (i, 0)),
                  pl.BlockSpec((META_ROWS, tm), lambda i, f: (0, i))],
        out_specs=pl.BlockSpec(memory_space=pl.ANY),
        scratch_shapes=[pltpu.VMEM((2, tm * TOK_ROWS, LANES), jnp.uint32),
                        pltpu.VMEM((MOE_TM * TOK_ROWS, LANES), jnp.uint32),
                        pltpu.SemaphoreType.DMA((2,)),
                        pltpu.SemaphoreType.DMA((1,))],
    )
    return pl.pallas_call(
        _dispatch_kernel,
        grid_spec=grid_spec,
        out_shape=jax.ShapeDtypeStruct((SORTED_TOKENS * TOK_ROWS, LANES), jnp.uint32),
        compiler_params=_cparams("arbitrary"),
        name="dispatch",
    )(fill_from, pos, x, meta)


def _moe_kernel(in_ref, elo_ref, ehi_ref, active_ref,
                xs_ref, w1a_ref, w3a_ref, w2a_ref, w1b_ref, w3b_ref, w2b_ref, ys_ref):
    j = pl.program_id(0)

    @pl.when(active_ref[j] != 0)
    def _():
        chunks = []
        for g in range(D_MODEL // (2 * LANES)):
            word = xs_ref[pl.ds(g, MOE_TM, stride=TOK_ROWS), :]
            lo = lax.bitcast_convert_type(lax.shift_left(word, jnp.uint32(16)), F32)
            hi = lax.bitcast_convert_type(word & jnp.uint32(HI_MASK), F32)
            chunks += [lo.astype(BF16), hi.astype(BF16)]
        xb = jnp.concatenate(chunks, axis=1)
        wts = lax.bitcast_convert_type(xs_ref[pl.ds(4, MOE_TM, stride=TOK_ROWS), :], F32)
        wa = wts[:, 0:1]
        wb = wts[:, 1:2]
        a1 = jnp.dot(xb, w1a_ref[...], preferred_element_type=F32)
        a3 = jnp.dot(xb, w3a_ref[...], preferred_element_type=F32)
        ha = ((a1 * jax.nn.sigmoid(a1)) * a3 * wa).astype(BF16)
        b1 = jnp.dot(xb, w1b_ref[...], preferred_element_type=F32)
        b3 = jnp.dot(xb, w3b_ref[...], preferred_element_type=F32)
        hb = ((b1 * jax.nn.sigmoid(b1)) * b3 * wb).astype(BF16)
        y = (jnp.dot(ha, w2a_ref[...], preferred_element_type=F32)
             + jnp.dot(hb, w2b_ref[...], preferred_element_type=F32))
        for k in range(N_LANE_CHUNKS):
            ys_ref[pl.ds(k, MOE_TM, stride=TOK_ROWS), :] = y[:, LANES * k:LANES * (k + 1)]

    @pl.when(active_ref[j] == 0)
    def _():
        ys_ref[...] = jnp.zeros_like(ys_ref)


def _moe_call(tables, xs, w1, w3, w2):
    in_tile, e_lo, e_hi, active = tables[2:]
    rows = MOE_TM * TOK_ROWS
    wspec = lambda shape, which: pl.BlockSpec(
        (None,) + shape, lambda j, ti, el, eh, ac: ((el, eh)[which][j], 0, 0))
    grid_spec = pltpu.PrefetchScalarGridSpec(
        num_scalar_prefetch=4,
        grid=(MOE_ITEMS,),
        in_specs=[pl.BlockSpec((rows, LANES), lambda j, ti, el, eh, ac: (ti[j], 0)),
                  wspec((D_MODEL, D_EXPERT), 0), wspec((D_MODEL, D_EXPERT), 0), wspec((D_EXPERT, D_MODEL), 0),
                  wspec((D_MODEL, D_EXPERT), 1), wspec((D_MODEL, D_EXPERT), 1), wspec((D_EXPERT, D_MODEL), 1)],
        out_specs=pl.BlockSpec((rows, LANES), lambda j, ti, el, eh, ac: (j, 0)),
    )
    return pl.pallas_call(
        _moe_kernel,
        grid_spec=grid_spec,
        out_shape=jax.ShapeDtypeStruct((MOE_ITEMS * rows, LANES), F32),
        compiler_params=_cparams("arbitrary"),
        name="moe",
    )(in_tile, e_lo, e_hi, active, xs, w1, w3, w2, w1, w3, w2)


def _ple_ln_kernel(pos_ref, pos_next_ref, x_ref, ys_hbm, p_ref, wpg_ref, bpg_ref, wp_ref, g_ref, b_ref,
                   o_ref, fbuf, sem):
    i = pl.program_id(0)
    n = pl.num_programs(0)
    tm = DISPATCH_TM
    slot = i % 2

    def row_gather(idx_ref, s):
        def body(r, carry):
            src = pl.multiple_of(idx_ref[0, r] * TOK_ROWS, TOK_ROWS)
            dst = pl.multiple_of(r * TOK_ROWS, TOK_ROWS)
            pltpu.make_async_copy(ys_hbm.at[pl.ds(src, TOK_ROWS)], fbuf.at[s, pl.ds(dst, TOK_ROWS)],
                                  sem.at[s]).start()
            return carry
        lax.fori_loop(0, tm, body, 0, unroll=8)

    @pl.when(i == 0)
    def _():
        row_gather(pos_ref, slot)

    @pl.when(i + 1 < n)
    def _():
        row_gather(pos_next_ref, 1 - slot)

    x = x_ref[...]
    gate = jax.nn.sigmoid(jnp.dot(x.astype(BF16), wpg_ref[...], preferred_element_type=F32) + bpg_ref[...])
    pe = jnp.dot(p_ref[...].astype(BF16), wp_ref[...], preferred_element_type=F32)
    base = DN_ALPHA * x + gate * pe

    pltpu.make_async_copy(ys_hbm.at[pl.ds(0, tm * TOK_ROWS)], fbuf.at[slot], sem.at[slot]).wait()
    ffn = jnp.concatenate([fbuf[slot, pl.ds(k, tm, stride=TOK_ROWS), :] for k in range(N_LANE_CHUNKS)], axis=1)
    o_ref[...] = _layer_norm(base + ffn, g_ref[...], b_ref[...])


def _ple_ln_call(pos, x, ys, p, wpg, bpg, wp, g, b):
    tm = DISPATCH_TM
    nt = TOKENS // tm
    row = lambda n: pl.BlockSpec((tm, n), lambda i: (i, 0))
    const = lambda *shape: pl.BlockSpec(shape, lambda i: (0,) * len(shape))
    return pl.pallas_call(
        _ple_ln_kernel,
        grid=(nt,),
        in_specs=[pl.BlockSpec((None, 1, tm), lambda i: (i, 0, 0), memory_space=pltpu.SMEM),
                  pl.BlockSpec((None, 1, tm), lambda i: (jnp.minimum(i + 1, nt - 1), 0, 0),
                               memory_space=pltpu.SMEM),
                  row(D_MODEL), pl.BlockSpec(memory_space=pl.ANY), row(D_PLE),
                  const(D_MODEL, D_MODEL), const(1, D_MODEL), const(D_PLE, D_MODEL),
                  const(1, D_MODEL), const(1, D_MODEL)],
        out_specs=row(D_MODEL),
        out_shape=jax.ShapeDtypeStruct((TOKENS, D_MODEL), F32),
        scratch_shapes=[pltpu.VMEM((2, tm * TOK_ROWS, LANES), F32),
                        pltpu.SemaphoreType.DMA((2,))],
        compiler_params=_cparams("arbitrary"),
        name="ple_ln",
    )(pos, pos, x, ys, p, wpg, bpg, wp, g, b)


def _dup_heads(w):
    lead = w.shape[:-1]
    w4 = w.reshape(lead + (N_KV_HEADS, 1, HEAD_DIM))
    return jnp.broadcast_to(w4, lead + (N_KV_HEADS, 2, HEAD_DIM)).reshape(lead + (2 * KV_CH,))


def _row(v):
    return v.reshape(1, -1)


def kernel(x, p, ln_emb_g, ln_emb_b, w_in, b_in, conv_w, conv_b, conv_ln_g, conv_ln_b, w_conv_out, w_attn_out, attn_sink, w_out, ln1_g, ln1_b, w_router_group, b_router_group, w_router_expert, b_router_expert, w1, w3, w2, w_p, w_pg, b_pg, ln2_g, ln2_b):
    c0, c1, c2, c3, c4 = (2 * CONV_CH, 2 * CONV_CH + D_MODEL, 2 * CONV_CH + D_MODEL + KV_CH,
                          2 * CONV_CH + D_MODEL + 2 * KV_CH, 2 * CONV_CH + 2 * D_MODEL + 2 * KV_CH)
    q_scale = HEAD_DIM ** -0.5
    router_tm = 512

    xs = _ln_call(x.reshape(TOKENS, D_MODEL), _row(ln_emb_g), _row(ln_emb_b))
    for i in range(DEPTH):
        w, b = w_in[i], b_in[i]
        wa = w[:, :CONV_CH].astype(BF16)
        wgl = w[:, CONV_CH:c0].astype(BF16)
        wq = (w[:, c0:c1] * q_scale).astype(BF16)
        bq = _row(b[c0:c1] * q_scale)
        wkv = jnp.concatenate([_dup_heads(w[:, c1:c2]), _dup_heads(w[:, c2:c3])], axis=1).astype(BF16)
        bkv = _row(jnp.concatenate([_dup_heads(b[c1:c2]), _dup_heads(b[c2:c3])]))
        wgate = w[:, c3:].astype(BF16)
        bgate = _row(b[c3:])
        cw = jnp.pad(conv_w[i], ((0, 1), (0, 0))).reshape(32, N_LANE_CHUNKS, LANES).transpose(1, 0, 2)

        yc = _glu_conv_call(xs, wa, wgl, _row(b[:CONV_CH]), _row(b[CONV_CH:c0]), cw, _row(conv_b[i]))
        kv = _kv_call(xs, wkv, bkv)
        x1 = _mixer_call(attn_sink[i].astype(F32), xs, yc, kv.reshape(BATCH, SEQ, 4 * KV_CH),
                         wq, bq, wgate, bgate, _row(conv_ln_g[i]), _row(conv_ln_b[i]),
                         w_conv_out[i].astype(BF16), w_attn_out[i].astype(BF16), w_out[i].astype(BF16),
                         _row(ln1_g[i]), _row(ln1_b[i]))

        wr = jnp.concatenate([w_router_group[i], w_router_expert[i]], axis=1).T
        wr = jnp.pad(wr, ((0, ROUTER_ROWS - wr.shape[0]), (0, 0)))
        wr_hi = wr.astype(BF16)
        wr_lo = (wr - wr_hi.astype(F32)).astype(BF16)
        br = jnp.pad(jnp.concatenate([b_router_group[i], b_router_expert[i]]), (0, ROUTER_ROWS - 20))
        br = jnp.broadcast_to(br[:, None], (ROUTER_ROWS, router_tm))
        meta, cnt = _router_call(x1, jnp.stack([wr_hi, wr_lo]), br, tm=router_tm)

        tables = _dispatch_tables(meta, cnt)
        sorted_x = _dispatch_call(tables, x1, meta)
        sorted_y = _moe_call(tables, sorted_x,
                             w1[i].reshape(N_EXPERTS, D_MODEL, D_EXPERT).astype(BF16),
                             w3[i].reshape(N_EXPERTS, D_MODEL, D_EXPERT).astype(BF16),
                             w2[i].reshape(N_EXPERTS, D_EXPERT, D_MODEL).astype(BF16))
        xs = _ple_ln_call(tables[0], x1, sorted_y, p[i].reshape(TOKENS, D_PLE), w_pg[i].astype(BF16),
                          _row(b_pg[i]), w_p[i].astype(BF16), _row(ln2_g[i]), _row(ln2_b[i]))
    return xs.reshape(BATCH, SEQ, D_MODEL)
```

```python
import functools

import jax
import jax.numpy as jnp
from jax import lax
from jax.experimental import pallas as pl
from jax.experimental.pallas import tpu as pltpu

D_MODEL = 1024
BATCH = 16
SEQ = 2048
DEPTH = 2
TOKENS = BATCH * SEQ
CONV_CH = D_MODEL
CONV_WIDTH = 31
CONV_PAD = CONV_WIDTH // 2
HEAD_DIM = 64
N_Q_HEADS = 16
N_KV_HEADS = 4
GQA_GROUP = 4
KV_CH = N_KV_HEADS * HEAD_DIM
WINDOW = 128
BLOCK = 128
SPAN = BLOCK + 2 * WINDOW
N_GROUPS = 4
EXPERTS_PER_GROUP = 4
N_EXPERTS = N_GROUPS * EXPERTS_PER_GROUP
D_EXPERT = 512
D_PLE = 256
DN_ALPHA = (2 * DEPTH) ** 0.25
LN_EPS = 1e-5
NEG_INF = -1e30

LANES = 128
N_LANE_CHUNKS = D_MODEL // LANES
VMEM_LIMIT = 56 * 1024 * 1024

F32 = jnp.float32
BF16 = jnp.bfloat16


def _layer_norm(v, g, b):
    mu = jnp.mean(v, axis=-1, keepdims=True)
    d = v - mu
    var = jnp.mean(d * d, axis=-1, keepdims=True)
    return d * lax.rsqrt(var + LN_EPS) * g + b


def _cparams(*sem):
    return pltpu.CompilerParams(dimension_semantics=sem, vmem_limit_bytes=VMEM_LIMIT)


def _ln_kernel(x_ref, g_ref, b_ref, o_ref):
    o_ref[...] = _layer_norm(x_ref[...], g_ref[...], b_ref[...])


def _ln_call(x, g, b, tm=512):
    return pl.pallas_call(
        _ln_kernel,
        grid=(TOKENS // tm,),
        in_specs=[pl.BlockSpec((tm, D_MODEL), lambda i: (i, 0)),
                  pl.BlockSpec((1, D_MODEL), lambda i: (0, 0)),
                  pl.BlockSpec((1, D_MODEL), lambda i: (0, 0))],
        out_specs=pl.BlockSpec((tm, D_MODEL), lambda i: (i, 0)),
        out_shape=jax.ShapeDtypeStruct((TOKENS, D_MODEL), F32),
        compiler_params=_cparams("arbitrary"),
        name="ln_emb",
    )(x, g, b)


GLU_BN = 256
GLU_PIECE = 512
CONV_RBW = 64
HALO = 16
PAIRS = (SEQ + 2 * HALO) // 2
HI_MASK = 0xFFFF0000


def _bits(v):
    return lax.bitcast_convert_type(v, jnp.uint32)


def _pack_pair(lo_bits, hi_bits):
    return lax.shift_right_logical(lo_bits, jnp.uint32(16)) | (hi_bits & jnp.uint32(HI_MASK))


def _glu_conv_kernel(x_ref, wa_ref, wg_ref, ba_ref, bg_ref, wkv_ref, bkv_ref, cw_ref, cb_ref,
                     y_ref, kv_ref, ubuf, pbuf, ybuf):
    n_chunks = GLU_BN // LANES
    n_pieces = SEQ // GLU_PIECE
    max_shift = (CONV_WIDTH - 1 + HALO - CONV_PAD) // 2
    for k in range(n_chunks):
        ubuf[k, 0:HALO, :] = jnp.zeros((HALO, LANES), F32)
        ubuf[k, HALO + SEQ:2 * HALO + SEQ, :] = jnp.zeros((HALO, LANES), F32)

    def project(i):
        rows = slice(GLU_PIECE * i, GLU_PIECE * (i + 1))
        xb = x_ref[rows, :].astype(BF16)
        a = jnp.dot(xb, wa_ref[...], preferred_element_type=F32) + ba_ref[...]
        g = jnp.dot(xb, wg_ref[...], preferred_element_type=F32) + bg_ref[...]
        u = (a * jax.nn.sigmoid(g)).astype(BF16).astype(F32)
        for k in range(n_chunks):
            ubuf[k, HALO + GLU_PIECE * i:HALO + GLU_PIECE * (i + 1), :] = u[:, LANES * k:LANES * (k + 1)]
        kv_ref[rows, :] = (jnp.dot(xb, wkv_ref[...], preferred_element_type=F32) + bkv_ref[...]).astype(BF16)

    def packed_upto(i):
        return PAIRS if i == n_pieces - 1 else GLU_PIECE // 2 * (i + 1)

    def pack(i):
        w0 = 0 if i == 0 else packed_upto(i - 1)
        n = packed_upto(i) - w0
        n_odd = n - 1 if i == n_pieces - 1 else n
        for k in range(n_chunks):
            even = _bits(ubuf[k, pl.ds(2 * w0, n, stride=2), :])
            odd = _bits(ubuf[k, pl.ds(2 * w0 + 1, n, stride=2), :])
            even_next = _bits(ubuf[k, pl.ds(2 * w0 + 2, n_odd, stride=2), :])
            pbuf[0, k, w0:w0 + n, :] = _pack_pair(even, odd)
            pbuf[1, k, w0:w0 + n_odd, :] = _pack_pair(odd[0:n_odd], even_next)

    def conv_block(rb):
        m0 = rb * CONV_RBW
        for k in range(n_chunks):
            cs = slice(LANES * k, LANES * (k + 1))
            acc = jnp.zeros((2 * CONV_RBW, LANES), BF16)
            for j in range(CONV_WIDTH):
                d = j + HALO - CONV_PAD
                word = pbuf[d % 2, k, m0 + d // 2:m0 + d // 2 + CONV_RBW, :]
                tap = pltpu.bitcast(jnp.broadcast_to(cw_ref[k, j:j + 1, :], (CONV_RBW, LANES)), BF16)
                acc = acc + pltpu.bitcast(word, BF16) * tap
            aw = pltpu.bitcast(acc, jnp.uint32)
            bias = cb_ref[:, cs]
            ybuf[k, pl.ds(2 * m0, CONV_RBW, stride=2), :] = (
                lax.bitcast_convert_type(lax.shift_left(aw, jnp.uint32(16)), F32) + bias)
            ybuf[k, pl.ds(2 * m0 + 1, CONV_RBW, stride=2), :] = (
                lax.bitcast_convert_type(aw & jnp.uint32(HI_MASK), F32) + bias)
            y_ref[2 * m0:2 * (m0 + CONV_RBW), cs] = ybuf[k, 2 * m0:2 * (m0 + CONV_RBW), :].astype(BF16)

    n_blocks = SEQ // 2 // CONV_RBW
    ready = [min(n_blocks, (packed_upto(i) - max_shift - CONV_RBW) // CONV_RBW + 1) for i in range(n_pieces)]
    project(0)
    pack(0)
    done = 0
    for i in range(1, n_pieces):
        project(i)
        for rb in range(done, ready[i - 1]):
            conv_block(rb)
        done = ready[i - 1]
        pack(i)
    for rb in range(done, n_blocks):
        conv_block(rb)


def _glu_conv_call(x, wa, wg, ba, bg, wkv, bkv, cw, cb):
    bn = GLU_BN
    n_chunks = bn // LANES
    col = lambda rows: pl.BlockSpec((rows, bn), lambda b, j: (0, j))
    return pl.pallas_call(
        _glu_conv_kernel,
        grid=(BATCH, CONV_CH // bn),
        in_specs=[pl.BlockSpec((SEQ, D_MODEL), lambda b, j: (b, 0)),
                  col(D_MODEL), col(D_MODEL), col(1), col(1), col(D_MODEL), col(1),
                  pl.BlockSpec((n_chunks, 32, LANES), lambda b, j: (j, 0, 0)),
                  col(1)],
        out_specs=[pl.BlockSpec((SEQ, bn), lambda b, j: (b, j)),
                   pl.BlockSpec((SEQ, bn), lambda b, j: (b, j))],
        out_shape=[jax.ShapeDtypeStruct((TOKENS, CONV_CH), BF16),
                   jax.ShapeDtypeStruct((TOKENS, 4 * KV_CH), BF16)],
        scratch_shapes=[pltpu.VMEM((n_chunks, SEQ + 2 * HALO, LANES), F32),
                        pltpu.VMEM((2, n_chunks, PAIRS, LANES), jnp.uint32),
                        pltpu.VMEM((n_chunks, SEQ, LANES), F32)],
        compiler_params=_cparams("arbitrary", "arbitrary"),
        name="glu_conv",
    )(x, wa, wg, ba, bg, wkv, bkv, cw, cb)


MIX_TS = 512
N_VARIANTS = 3


def _alibi_slope(h):
    return 2.0 ** (-8.0 * (h + 1) / N_Q_HEADS)


def _mixer_kernel(sink_ref,
                  x_ref, yc_ref, kv_ref, wq_ref, bq_ref, wg_ref, bg_ref,
                  clg_ref, clb_ref,
                  wco_ref, wao_ref, wo_ref, l1g_ref, l1b_ref,
                  o_ref,
                  obuf, bias_tab):
    b = pl.program_id(0)
    s = pl.program_id(1)
    ts = MIX_TS
    t0 = pl.multiple_of(s * ts, ts)

    @pl.when((b == 0) & (s == 0))
    def _():
        r = lax.broadcasted_iota(jnp.int32, (BLOCK, SPAN), 0)
        kk = lax.broadcasted_iota(jnp.int32, (BLOCK, SPAN), 1)
        for v in range(N_VARIANTS):
            dist = jnp.abs(kk - r - v * WINDOW)
            inside = dist <= WINDOW
            distf = dist.astype(F32)
            for h in range(N_Q_HEADS):
                bias_tab[v * N_Q_HEADS + h] = jnp.where(inside, -_alibi_slope(h) * distf, NEG_INF)

    yn = _layer_norm(yc_ref[...].astype(F32), clg_ref[...], clb_ref[...])
    act = (yn * jax.nn.sigmoid(yn)).astype(BF16)
    y_conv = jnp.dot(act, wco_ref[...], preferred_element_type=F32)

    xt = x_ref[...]
    xb = xt.astype(BF16)
    q = (jnp.dot(xb, wq_ref[...], preferred_element_type=F32) + bq_ref[...]).astype(BF16)
    low_half = lax.broadcasted_iota(jnp.int32, (BLOCK, LANES), 1) < HEAD_DIM
    zero_q = jnp.zeros((BLOCK, LANES), BF16)
    nt_dims = (((1,), (1,)), ((), ()))
    for qi in range(ts // BLOCK):
        rows = slice(BLOCK * qi, BLOCK * (qi + 1))
        qs = t0 + BLOCK * qi
        ws = pl.multiple_of(jnp.clip(qs - WINDOW, 0, SEQ - SPAN), BLOCK)
        variant = jnp.where(qs == 0, 0, jnp.where(qs == SEQ - BLOCK, 2, 1))
        for h in range(N_KV_HEADS):
            kw = kv_ref[pl.ds(ws, SPAN), LANES * h:LANES * (h + 1)]
            vw = kv_ref[pl.ds(ws, SPAN), 2 * KV_CH + LANES * h:2 * KV_CH + LANES * (h + 1)]
            qc = [q[rows, 2 * LANES * h:2 * LANES * h + LANES],
                  q[rows, 2 * LANES * h + LANES:2 * LANES * (h + 1)]]
            qst = jnp.concatenate([jnp.where(low_half, qc[0], zero_q), jnp.where(low_half, zero_q, qc[0]),
                                   jnp.where(low_half, qc[1], zero_q), jnp.where(low_half, zero_q, qc[1])], axis=0)
            sc = lax.dot_general(qst, kw, nt_dims, preferred_element_type=F32)
            probs = []
            inv_den = []
            for g in range(GQA_GROUP):
                hq = GQA_GROUP * h + g
                sg = sc[BLOCK * g:BLOCK * (g + 1), :] + bias_tab[variant * N_Q_HEADS + hq]
                sink = sink_ref[hq]
                m = jnp.maximum(jnp.max(sg, axis=-1, keepdims=True), sink)
                p = jnp.exp(sg - m)
                den = jnp.sum(p, axis=-1, keepdims=True) + jnp.exp(sink - m)
                probs.append(p.astype(BF16))
                inv_den.append(den)
            pv = jnp.dot(jnp.concatenate(probs, axis=0), vw, preferred_element_type=F32)
            og = [pv[BLOCK * g:BLOCK * (g + 1), :] / inv_den[g] for g in range(GQA_GROUP)]
            obuf[rows, 2 * LANES * h:2 * LANES * h + LANES] = jnp.where(low_half, og[0], og[1]).astype(BF16)
            obuf[rows, 2 * LANES * h + LANES:2 * LANES * (h + 1)] = jnp.where(low_half, og[2], og[3]).astype(BF16)
    y_attn = jnp.dot(obuf[...], wao_ref[...], preferred_element_type=F32)

    gates = jax.nn.sigmoid(jnp.dot(xb, wg_ref[...], preferred_element_type=F32) + bg_ref[...])
    merged = gates[:, :D_MODEL] * y_conv + gates[:, D_MODEL:] * y_attn
    out = jnp.dot(merged.astype(BF16), wo_ref[...], preferred_element_type=F32)
    o_ref[...] = _layer_norm(DN_ALPHA * xt + out, l1g_ref[...], l1b_ref[...])


def _mixer_call(sink, x, yc, kv, wq, bq, wg, bg, clg, clb, wco, wao, wo, l1g, l1b):
    ts = MIX_TS
    ns = SEQ // ts
    const = lambda *shape: pl.BlockSpec(shape, lambda b, s, sk: (0,) * len(shape),
                                        pipeline_mode=pl.Buffered(1))
    grid_spec = pltpu.PrefetchScalarGridSpec(
        num_scalar_prefetch=1,
        grid=(BATCH, ns),
        in_specs=[pl.BlockSpec((ts, D_MODEL), lambda b, s, sk: (b * ns + s, 0)),
                  pl.BlockSpec((ts, CONV_CH), lambda b, s, sk: (b * ns + s, 0)),
                  pl.BlockSpec((None, SEQ, 4 * KV_CH), lambda b, s, sk: (b, 0, 0),
                               pipeline_mode=pl.Buffered(1)),
                  const(D_MODEL, D_MODEL), const(1, D_MODEL),
                  const(D_MODEL, 2 * D_MODEL), const(1, 2 * D_MODEL),
                  const(1, CONV_CH), const(1, CONV_CH),
                  const(CONV_CH, D_MODEL), const(D_MODEL, D_MODEL), const(D_MODEL, D_MODEL),
                  const(1, D_MODEL), const(1, D_MODEL)],
        out_specs=pl.BlockSpec((ts, D_MODEL), lambda b, s, sk: (b * ns + s, 0)),
        scratch_shapes=[pltpu.VMEM((ts, D_MODEL), BF16),
                        pltpu.VMEM((N_VARIANTS * N_Q_HEADS, BLOCK, SPAN), F32)],
    )
    return pl.pallas_call(
        _mixer_kernel,
        grid_spec=grid_spec,
        out_shape=jax.ShapeDtypeStruct((TOKENS, D_MODEL), F32),
        compiler_params=_cparams("arbitrary", "arbitrary"),
        name="mixer",
    )(sink, x, yc, kv, wq, bq, wg, bg, clg, clb, wco, wao, wo, l1g, l1b)


ROUTER_ROWS = 32


N_PAIRS = 6
N_CLASSES = N_GROUPS * N_PAIRS
PAIR_LO = (0, 0, 0, 1, 1, 2)
PAIR_HI = (1, 2, 3, 2, 3, 3)
META_ROWS = 8


def _router_kernel(x_ref, w_ref, b_ref, meta_ref, cnt_ref, carry_ref):
    @pl.when(pl.program_id(0) == 0)
    def _():
        carry_ref[...] = jnp.zeros_like(carry_ref)

    x = x_ref[...]
    tm = x.shape[0]
    xh = x.astype(BF16)
    xl = (x - xh.astype(F32)).astype(BF16)
    nt_dims = (((1,), (1,)), ((), ()))
    lt = (lax.dot_general(w_ref[0], xh, nt_dims, preferred_element_type=F32)
          + lax.dot_general(w_ref[1], xh, nt_dims, preferred_element_type=F32)
          + lax.dot_general(w_ref[0], xl, nt_dims, preferred_element_type=F32))
    lt = lt + b_ref[...]
    row4 = lax.broadcasted_iota(jnp.int32, (N_GROUPS, tm), 0)
    gl = lt[0:N_GROUPS, :]
    gmax = jnp.max(gl, axis=0, keepdims=True)
    g_w = 1.0 / jnp.sum(jnp.exp(gl - gmax), axis=0, keepdims=True)
    g_idx = jnp.min(jnp.where(gl == gmax, row4, N_GROUPS), axis=0, keepdims=True)
    e_sel = jnp.zeros((EXPERTS_PER_GROUP, tm), F32)
    for g in range(N_GROUPS):
        lo = N_GROUPS + EXPERTS_PER_GROUP * g
        e_sel = e_sel + jnp.where(g_idx == g, lt[lo:lo + EXPERTS_PER_GROUP, :], 0.0)
    e1 = jnp.max(e_sel, axis=0, keepdims=True)
    i1 = jnp.min(jnp.where(e_sel == e1, row4, EXPERTS_PER_GROUP), axis=0, keepdims=True)
    rest = jnp.where(row4 == i1, -jnp.inf, e_sel)
    e2 = jnp.max(rest, axis=0, keepdims=True)
    i2 = jnp.min(jnp.where(rest == e2, row4, EXPERTS_PER_GROUP), axis=0, keepdims=True)
    t = jnp.exp(e2 - e1)
    w1 = (1.0 / (1.0 + t)) * g_w
    w2 = (t / (1.0 + t)) * g_w
    first_lower = i1 < i2
    e_lo = jnp.minimum(i1, i2)
    e_hi = jnp.maximum(i1, i2)
    w_lo = jnp.where(first_lower, w1, w2)
    w_hi = jnp.where(first_lower, w2, w1)
    pair = jnp.where(e_lo == 0, 0, jnp.where(e_lo == 1, 3, 5)) + e_hi - e_lo - 1
    cls = g_idx * N_PAIRS + pair

    row32 = lax.broadcasted_iota(jnp.int32, (ROUTER_ROWS, tm), 0)
    member = row32 == cls
    onehot = jnp.where(member, 1.0, 0.0)
    src = lax.broadcasted_iota(jnp.int32, (tm, tm), 0)
    dst = lax.broadcasted_iota(jnp.int32, (tm, tm), 1)
    upper = jnp.where(src <= dst, 1.0, 0.0).astype(BF16)
    prefix = jnp.dot(onehot.astype(BF16), upper, preferred_element_type=F32)
    carry = carry_ref[:, 0:1]
    rank = jnp.sum(jnp.where(member, prefix - 1.0 + carry, 0.0), axis=0, keepdims=True)
    carry_ref[...] = carry_ref[...] + jnp.sum(onehot, axis=1, keepdims=True)
    cnt_ref[...] = carry_ref[...]

    meta_ref[...] = jnp.concatenate(
        [cls.astype(F32), rank, w_lo, w_hi, jnp.zeros((META_ROWS - 4, tm), F32)], axis=0)


def _router_call(x, w, b, tm=512):
    return pl.pallas_call(
        _router_kernel,
        grid=(TOKENS // tm,),
        in_specs=[pl.BlockSpec((tm, D_MODEL), lambda i: (i, 0)),
                  pl.BlockSpec((2, ROUTER_ROWS, D_MODEL), lambda i: (0, 0, 0)),
                  pl.BlockSpec((ROUTER_ROWS, tm), lambda i: (0, 0))],
        out_specs=[pl.BlockSpec((META_ROWS, tm), lambda i: (0, i)),
                   pl.BlockSpec((ROUTER_ROWS, LANES), lambda i: (0, 0))],
        out_shape=[jax.ShapeDtypeStruct((META_ROWS, TOKENS), F32),
                   jax.ShapeDtypeStruct((ROUTER_ROWS, LANES), F32)],
        scratch_shapes=[pltpu.VMEM((ROUTER_ROWS, LANES), F32)],
        compiler_params=_cparams("arbitrary"),
        name="router",
    )(x, w, b)


MOE_TM = 256
TOK_ROWS = 8
MOE_ITEMS = TOKENS // MOE_TM + N_CLASSES
SORTED_TILES = MOE_ITEMS + 1
SORTED_TOKENS = SORTED_TILES * MOE_TM
MAX_IDLE_TILES = SORTED_TILES - TOKENS // MOE_TM
DISPATCH_TM = 512
DMA_QUEUES = 2


def _dispatch_tables(meta, cnt):
    cls = meta[0].astype(jnp.int32)
    rank = meta[1].astype(jnp.int32)
    counts = cnt[:N_CLASSES, 0].astype(jnp.int32)
    tiles_per_class = (counts + MOE_TM - 1) // MOE_TM
    tile_end = jnp.cumsum(tiles_per_class)
    starts = (tile_end - tiles_per_class) * MOE_TM
    pos = starts[cls] + rank
    fill_from = starts + counts
    n_tiles = tile_end[-1]
    j = jnp.arange(MOE_ITEMS, dtype=jnp.int32)
    active = j < n_tiles
    jj = jnp.minimum(j, n_tiles - 1)
    c = jnp.minimum(jnp.sum(tile_end[None, :] <= jj[:, None], axis=1), N_CLASSES - 1)
    grp, pair = c // N_PAIRS, c % N_PAIRS
    e_lo = grp * EXPERTS_PER_GROUP + jnp.asarray(PAIR_LO, jnp.int32)[pair]
    e_hi = grp * EXPERTS_PER_GROUP + jnp.asarray(PAIR_HI, jnp.int32)[pair]
    i32 = lambda v: v.astype(jnp.int32)
    fill = jnp.concatenate([i32(fill_from), i32(n_tiles).reshape(1)])
    return (i32(pos).reshape(TOKENS // DISPATCH_TM, 1, DISPATCH_TM), fill,
            i32(jj), i32(e_lo), i32(e_hi), i32(active))


def _dispatch_kernel(fill_ref, pos_ref, x_ref, meta_ref, xs_hbm, stage, zeros, sem, zsem):
    i = pl.program_id(0)
    n = pl.num_programs(0)
    tm = DISPATCH_TM
    slot = i % 2

    def slab_wait(s):
        pltpu.make_async_copy(stage.at[s], xs_hbm.at[pl.ds(0, tm * TOK_ROWS)], sem.at[s]).wait()

    @pl.when(i == 0)
    def _():
        zeros[...] = jnp.zeros_like(zeros)
        for c in range(N_CLASSES):
            dst = pl.multiple_of(fill_ref[c] * TOK_ROWS, TOK_ROWS)
            pltpu.make_async_copy(zeros, xs_hbm.at[pl.ds(dst, MOE_TM * TOK_ROWS)], zsem.at[0]).start()
        for c in range(N_CLASSES):
            pltpu.make_async_copy(zeros, xs_hbm.at[pl.ds(0, MOE_TM * TOK_ROWS)], zsem.at[0]).wait()
        for k in range(MAX_IDLE_TILES):
            t = fill_ref[N_CLASSES] + k

            @pl.when(t < SORTED_TILES)
            def _():
                dst = pl.multiple_of(t * (MOE_TM * TOK_ROWS), MOE_TM * TOK_ROWS)
                tail = pltpu.make_async_copy(zeros, xs_hbm.at[pl.ds(dst, MOE_TM * TOK_ROWS)], zsem.at[0])
                tail.start()
                tail.wait()

    @pl.when(i >= 2)
    def _():
        slab_wait(slot)

    x = x_ref[...]
    for g in range(D_MODEL // (2 * LANES)):
        lo = x[:, 2 * LANES * g:2 * LANES * g + LANES].astype(BF16).astype(F32)
        hi = x[:, 2 * LANES * g + LANES:2 * LANES * (g + 1)].astype(BF16).astype(F32)
        word = (lax.shift_right_logical(lax.bitcast_convert_type(lo, jnp.uint32), jnp.uint32(16))
                | (lax.bitcast_convert_type(hi, jnp.uint32) & jnp.uint32(HI_MASK)))
        stage[slot, pl.ds(g, tm, stride=TOK_ROWS), :] = word
    wrows = jnp.concatenate([meta_ref[2:4, :], jnp.zeros((LANES - 2, tm), F32)], axis=0)
    stage[slot, pl.ds(4, tm, stride=TOK_ROWS), :] = lax.bitcast_convert_type(wrows.T, jnp.uint32)
    for g in range(5, TOK_ROWS):
        stage[slot, pl.ds(g, tm, stride=TOK_ROWS), :] = jnp.zeros((tm, LANES), jnp.uint32)

    def body(h, carry):
        for queue in range(DMA_QUEUES):
            r = h * DMA_QUEUES + queue
            dst = pl.multiple_of(pos_ref[0, r] * TOK_ROWS, TOK_ROWS)
            src = pl.multiple_of(r * TOK_ROWS, TOK_ROWS)
            pltpu.make_async_copy(stage.at[slot, pl.ds(src, TOK_ROWS)], xs_hbm.at[pl.ds(dst, TOK_ROWS)],
                                  sem.at[slot]).start(priority=queue)
        return carry
    lax.fori_loop(0, tm // DMA_QUEUES, body, 0, unroll=4)

    @pl.when(i == n - 1)
    def _():
        slab_wait(1 - slot)
        slab_wait(slot)


def _dispatch_call(tables, x, meta):
    pos, fill_from = tables[0], tables[1]
    tm = DISPATCH_TM
    grid_spec = pltpu.PrefetchScalarGridSpec(
        num_scalar_prefetch=1,
        grid=(TOKENS // tm,),
        in_specs=[pl.BlockSpec((None, 1, tm), lambda i, f: (i, 0, 0), memory_space=pltpu.SMEM),
                  pl.BlockSpec((tm, D_MODEL), lambda i, f: (i, 0)),
                  pl.BlockSpec((META_ROWS, tm), lambda i, f: (0, i))],
        out_specs=pl.BlockSpec(memory_space=pl.ANY),
        scratch_shapes=[pltpu.VMEM((2, tm * TOK_ROWS, LANES), jnp.uint32),
                        pltpu.VMEM((MOE_TM * TOK_ROWS, LANES), jnp.uint32),
                        pltpu.SemaphoreType.DMA((2,)),
                        pltpu.SemaphoreType.DMA((1,))],
    )
    return pl.pallas_call(
        _dispatch_kernel,
        grid_spec=grid_spec,
        out_shape=jax.ShapeDtypeStruct((SORTED_TOKENS * TOK_ROWS, LANES), jnp.uint32),
        compiler_params=_cparams("arbitrary"),
        name="dispatch",
    )(fill_from, pos, x, meta)


def _moe_kernel(in_ref, elo_ref, ehi_ref, active_ref,
                xs_ref, w1a_ref, w3a_ref, w2a_ref, w1b_ref, w3b_ref, w2b_ref, ys_ref):
    j = pl.program_id(0)

    @pl.when(active_ref[j] != 0)
    def _():
        chunks = []
        for g in range(D_MODEL // (2 * LANES)):
            word = xs_ref[pl.ds(g, MOE_TM, stride=TOK_ROWS), :]
            lo = lax.bitcast_convert_type(lax.shift_left(word, jnp.uint32(16)), F32)
            hi = lax.bitcast_convert_type(word & jnp.uint32(HI_MASK), F32)
            chunks += [lo.astype(BF16), hi.astype(BF16)]
        xb = jnp.concatenate(chunks, axis=1)
        wts = lax.bitcast_convert_type(xs_ref[pl.ds(4, MOE_TM, stride=TOK_ROWS), :], F32)
        wa = wts[:, 0:1]
        wb = wts[:, 1:2]
        a1 = jnp.dot(xb, w1a_ref[...], preferred_element_type=F32)
        a3 = jnp.dot(xb, w3a_ref[...], preferred_element_type=F32)
        ha = ((a1 * jax.nn.sigmoid(a1)) * a3 * wa).astype(BF16)
        b1 = jnp.dot(xb, w1b_ref[...], preferred_element_type=F32)
        b3 = jnp.dot(xb, w3b_ref[...], preferred_element_type=F32)
        hb = ((b1 * jax.nn.sigmoid(b1)) * b3 * wb).astype(BF16)
        y = (jnp.dot(ha, w2a_ref[...], preferred_element_type=F32)
             + jnp.dot(hb, w2b_ref[...], preferred_element_type=F32))
        for k in range(N_LANE_CHUNKS):
            ys_ref[pl.ds(k, MOE_TM, stride=TOK_ROWS), :] = y[:, LANES * k:LANES * (k + 1)]

    @pl.when(active_ref[j] == 0)
    def _():
        ys_ref[...] = jnp.zeros_like(ys_ref)


def _moe_call(tables, xs, w1, w3, w2):
    in_tile, e_lo, e_hi, active = tables[2:]
    rows = MOE_TM * TOK_ROWS
    wspec = lambda shape, which: pl.BlockSpec(
        (None,) + shape, lambda j, ti, el, eh, ac: ((el, eh)[which][j], 0, 0))
    grid_spec = pltpu.PrefetchScalarGridSpec(
        num_scalar_prefetch=4,
        grid=(MOE_ITEMS,),
        in_specs=[pl.BlockSpec((rows, LANES), lambda j, ti, el, eh, ac: (ti[j], 0)),
                  wspec((D_MODEL, D_EXPERT), 0), wspec((D_MODEL, D_EXPERT), 0), wspec((D_EXPERT, D_MODEL), 0),
                  wspec((D_MODEL, D_EXPERT), 1), wspec((D_MODEL, D_EXPERT), 1), wspec((D_EXPERT, D_MODEL), 1)],
        out_specs=pl.BlockSpec((rows, LANES), lambda j, ti, el, eh, ac: (j, 0)),
    )
    return pl.pallas_call(
        _moe_kernel,
        grid_spec=grid_spec,
        out_shape=jax.ShapeDtypeStruct((MOE_ITEMS * rows, LANES), F32),
        compiler_params=_cparams("arbitrary"),
        name="moe",
    )(in_tile, e_lo, e_hi, active, xs, w1, w3, w2, w1, w3, w2)


def _ple_ln_kernel(pos_ref, pos_next_ref, x_ref, ys_hbm, p_ref, wpg_ref, bpg_ref, wp_ref, g_ref, b_ref,
                   o_ref, fbuf, sem):
    i = pl.program_id(0)
    n = pl.num_programs(0)
    tm = DISPATCH_TM
    slot = i % 2

    def row_gather(idx_ref, s):
        def body(h, carry):
            for queue in range(DMA_QUEUES):
                r = h * DMA_QUEUES + queue
                src = pl.multiple_of(idx_ref[0, r] * TOK_ROWS, TOK_ROWS)
                dst = pl.multiple_of(r * TOK_ROWS, TOK_ROWS)
                pltpu.make_async_copy(ys_hbm.at[pl.ds(src, TOK_ROWS)], fbuf.at[s, pl.ds(dst, TOK_ROWS)],
                                      sem.at[s]).start(priority=queue)
            return carry
        lax.fori_loop(0, tm // DMA_QUEUES, body, 0, unroll=4)

    @pl.when(i == 0)
    def _():
        row_gather(pos_ref, slot)

    @pl.when(i + 1 < n)
    def _():
        row_gather(pos_next_ref, 1 - slot)

    x = x_ref[...]
    gate = jax.nn.sigmoid(jnp.dot(x.astype(BF16), wpg_ref[...], preferred_element_type=F32) + bpg_ref[...])
    pe = jnp.dot(p_ref[...].astype(BF16), wp_ref[...], preferred_element_type=F32)
    base = DN_ALPHA * x + gate * pe

    pltpu.make_async_copy(ys_hbm.at[pl.ds(0, tm * TOK_ROWS)], fbuf.at[slot], sem.at[slot]).wait()
    ffn = jnp.concatenate([fbuf[slot, pl.ds(k, tm, stride=TOK_ROWS), :] for k in range(N_LANE_CHUNKS)], axis=1)
    o_ref[...] = _layer_norm(base + ffn, g_ref[...], b_ref[...])


def _ple_ln_call(pos, x, ys, p, wpg, bpg, wp, g, b):
    tm = DISPATCH_TM
    nt = TOKENS // tm
    row = lambda n: pl.BlockSpec((tm, n), lambda i: (i, 0))
    const = lambda *shape: pl.BlockSpec(shape, lambda i: (0,) * len(shape))
    return pl.pallas_call(
        _ple_ln_kernel,
        grid=(nt,),
        in_specs=[pl.BlockSpec((None, 1, tm), lambda i: (i, 0, 0), memory_space=pltpu.SMEM),
                  pl.BlockSpec((None, 1, tm), lambda i: (jnp.minimum(i + 1, nt - 1), 0, 0),
                               memory_space=pltpu.SMEM),
                  row(D_MODEL), pl.BlockSpec(memory_space=pl.ANY), row(D_PLE),
                  const(D_MODEL, D_MODEL), const(1, D_MODEL), const(D_PLE, D_MODEL),
                  const(1, D_MODEL), const(1, D_MODEL)],
        out_specs=row(D_MODEL),
        out_shape=jax.ShapeDtypeStruct((TOKENS, D_MODEL), F32),
        scratch_shapes=[pltpu.VMEM((2, tm * TOK_ROWS, LANES), F32),
                        pltpu.SemaphoreType.DMA((2,))],
        compiler_params=_cparams("arbitrary"),
        name="ple_ln",
    )(pos, pos, x, ys, p, wpg, bpg, wp, g, b)


def _dup_heads(w):
    lead = w.shape[:-1]
    w4 = w.reshape(lead + (N_KV_HEADS, 1, HEAD_DIM))
    return jnp.broadcast_to(w4, lead + (N_KV_HEADS, 2, HEAD_DIM)).reshape(lead + (2 * KV_CH,))


def _row(v):
    return v.reshape(1, -1)


def kernel(x, p, ln_emb_g, ln_emb_b, w_in, b_in, conv_w, conv_b, conv_ln_g, conv_ln_b, w_conv_out, w_attn_out, attn_sink, w_out, ln1_g, ln1_b, w_router_group, b_router_group, w_router_expert, b_router_expert, w1, w3, w2, w_p, w_pg, b_pg, ln2_g, ln2_b):
    c0, c1, c2, c3, c4 = (2 * CONV_CH, 2 * CONV_CH + D_MODEL, 2 * CONV_CH + D_MODEL + KV_CH,
                          2 * CONV_CH + D_MODEL + 2 * KV_CH, 2 * CONV_CH + 2 * D_MODEL + 2 * KV_CH)
    q_scale = HEAD_DIM ** -0.5
    router_tm = 512

    xs = _ln_call(x.reshape(TOKENS, D_MODEL), _row(ln_emb_g), _row(ln_emb_b))
    for i in range(DEPTH):
        w, b = w_in[i], b_in[i]
        wa = w[:, :CONV_CH].astype(BF16)
        wgl = w[:, CONV_CH:c0].astype(BF16)
        wq = (w[:, c0:c1] * q_scale).astype(BF16)
        bq = _row(b[c0:c1] * q_scale)
        wkv = jnp.concatenate([_dup_heads(w[:, c1:c2]), _dup_heads(w[:, c2:c3])], axis=1).astype(BF16)
        bkv = _row(jnp.concatenate([_dup_heads(b[c1:c2]), _dup_heads(b[c2:c3])]))
        wgate = w[:, c3:].astype(BF16)
        bgate = _row(b[c3:])
        cw16 = lax.bitcast_convert_type(jnp.pad(conv_w[i], ((0, 1), (0, 0))).astype(BF16), jnp.uint16)
        cw = cw16.astype(jnp.uint32) * jnp.uint32(0x00010001)
        cw = cw.reshape(32, N_LANE_CHUNKS, LANES).transpose(1, 0, 2)

        yc, kv = _glu_conv_call(xs, wa, wgl, _row(b[:CONV_CH]), _row(b[CONV_CH:c0]), wkv, bkv,
                                cw, _row(conv_b[i]))
        x1 = _mixer_call(attn_sink[i].astype(F32), xs, yc, kv.reshape(BATCH, SEQ, 4 * KV_CH),
                         wq, bq, wgate, bgate, _row(conv_ln_g[i]), _row(conv_ln_b[i]),
                         w_conv_out[i].astype(BF16), w_attn_out[i].astype(BF16), w_out[i].astype(BF16),
                         _row(ln1_g[i]), _row(ln1_b[i]))

        wr = jnp.concatenate([w_router_group[i], w_router_expert[i]], axis=1).T
        wr = jnp.pad(wr, ((0, ROUTER_ROWS - wr.shape[0]), (0, 0)))
        wr_hi = wr.astype(BF16)
        wr_lo = (wr - wr_hi.astype(F32)).astype(BF16)
        br = jnp.pad(jnp.concatenate([b_router_group[i], b_router_expert[i]]), (0, ROUTER_ROWS - 20))
        br = jnp.broadcast_to(br[:, None], (ROUTER_ROWS, router_tm))
        meta, cnt = _router_call(x1, jnp.stack([wr_hi, wr_lo]), br, tm=router_tm)

        tables = _dispatch_tables(meta, cnt)
        sorted_x = _dispatch_call(tables, x1, meta)
        sorted_y = _moe_call(tables, sorted_x,
                             w1[i].reshape(N_EXPERTS, D_MODEL, D_EXPERT).astype(BF16),
                             w3[i].reshape(N_EXPERTS, D_MODEL, D_EXPERT).astype(BF16),
                             w2[i].reshape(N_EXPERTS, D_EXPERT, D_MODEL).astype(BF16))
        xs = _ple_ln_call(tables[0], x1, sorted_y, p[i].reshape(TOKENS, D_PLE), w_pg[i].astype(BF16),
                          _row(b_pg[i]), w_p[i].astype(BF16), _row(ln2_g[i]), _row(ln2_b[i]))
    return xs.reshape(BATCH, SEQ, D_MODEL)
```

```python
import functools

import jax
import jax.numpy as jnp
from jax import lax
from jax.experimental import pallas as pl
from jax.experimental.pallas import tpu as pltpu

D_MODEL = 1024
BATCH = 16
SEQ = 2048
DEPTH = 2
TOKENS = BATCH * SEQ
CONV_CH = D_MODEL
CONV_WIDTH = 31
CONV_PAD = CONV_WIDTH // 2
HEAD_DIM = 64
N_Q_HEADS = 16
N_KV_HEADS = 4
GQA_GROUP = 4
KV_CH = N_KV_HEADS * HEAD_DIM
WINDOW = 128
BLOCK = 128
SPAN = BLOCK + 2 * WINDOW
N_GROUPS = 4
EXPERTS_PER_GROUP = 4
N_EXPERTS = N_GROUPS * EXPERTS_PER_GROUP
D_EXPERT = 512
D_PLE = 256
DN_ALPHA = (2 * DEPTH) ** 0.25
LN_EPS = 1e-5
NEG_INF = -1e30

LANES = 128
N_LANE_CHUNKS = D_MODEL // LANES
VMEM_LIMIT = 56 * 1024 * 1024

F32 = jnp.float32
BF16 = jnp.bfloat16


def _layer_norm(v, g, b):
    mu = jnp.mean(v, axis=-1, keepdims=True)
    d = v - mu
    var = jnp.mean(d * d, axis=-1, keepdims=True)
    return d * lax.rsqrt(var + LN_EPS) * g + b


def _cparams(*sem):
    return pltpu.CompilerParams(dimension_semantics=sem, vmem_limit_bytes=VMEM_LIMIT)


def _ln_kernel(x_ref, g_ref, b_ref, o_ref):
    o_ref[...] = _layer_norm(x_ref[...], g_ref[...], b_ref[...])


def _ln_call(x, g, b, tm=512):
    return pl.pallas_call(
        _ln_kernel,
        grid=(TOKENS // tm,),
        in_specs=[pl.BlockSpec((tm, D_MODEL), lambda i: (i, 0)),
                  pl.BlockSpec((1, D_MODEL), lambda i: (0, 0)),
                  pl.BlockSpec((1, D_MODEL), lambda i: (0, 0))],
        out_specs=pl.BlockSpec((tm, D_MODEL), lambda i: (i, 0)),
        out_shape=jax.ShapeDtypeStruct((TOKENS, D_MODEL), F32),
        compiler_params=_cparams("arbitrary"),
        name="ln_emb",
    )(x, g, b)


GLU_BN = 256
GLU_PIECE = 512
CONV_RBW = 64
HALO = 16
PAIRS = (SEQ + 2 * HALO) // 2
HI_MASK = 0xFFFF0000


def _bits(v):
    return lax.bitcast_convert_type(v, jnp.uint32)


def _pack_pair(lo_bits, hi_bits):
    return lax.shift_right_logical(lo_bits, jnp.uint32(16)) | (hi_bits & jnp.uint32(HI_MASK))


def _glu_conv_kernel(x_ref, wa_ref, wg_ref, ba_ref, bg_ref, wkv_ref, bkv_ref, cw_ref, cb_ref,
                     y_ref, kv_ref, ubuf, pbuf, ybuf):
    n_chunks = GLU_BN // LANES
    n_pieces = SEQ // GLU_PIECE
    max_shift = (CONV_WIDTH - 1 + HALO - CONV_PAD) // 2
    for k in range(n_chunks):
        ubuf[k, 0:HALO, :] = jnp.zeros((HALO, LANES), F32)
        ubuf[k, HALO + SEQ:2 * HALO + SEQ, :] = jnp.zeros((HALO, LANES), F32)

    def project(i, between):
        rows = slice(GLU_PIECE * i, GLU_PIECE * (i + 1))
        xb = x_ref[rows, :].astype(BF16)
        a = jnp.dot(xb, wa_ref[...], preferred_element_type=F32) + ba_ref[...]
        for rb in between[0::3]:
            conv_block(rb)
        g = jnp.dot(xb, wg_ref[...], preferred_element_type=F32) + bg_ref[...]
        for rb in between[1::3]:
            conv_block(rb)
        u = (a * jax.nn.sigmoid(g)).astype(BF16).astype(F32)
        for k in range(n_chunks):
            ubuf[k, HALO + GLU_PIECE * i:HALO + GLU_PIECE * (i + 1), :] = u[:, LANES * k:LANES * (k + 1)]
        kv_ref[rows, :] = (jnp.dot(xb, wkv_ref[...], preferred_element_type=F32) + bkv_ref[...]).astype(BF16)
        for rb in between[2::3]:
            conv_block(rb)

    def packed_upto(i):
        return PAIRS if i == n_pieces - 1 else GLU_PIECE // 2 * (i + 1)

    def pack(i):
        w0 = 0 if i == 0 else packed_upto(i - 1)
        n = packed_upto(i) - w0
        n_odd = n - 1 if i == n_pieces - 1 else n
        for k in range(n_chunks):
            even = _bits(ubuf[k, pl.ds(2 * w0, n, stride=2), :])
            odd = _bits(ubuf[k, pl.ds(2 * w0 + 1, n, stride=2), :])
            even_next = _bits(ubuf[k, pl.ds(2 * w0 + 2, n_odd, stride=2), :])
            pbuf[0, k, w0:w0 + n, :] = _pack_pair(even, odd)
            pbuf[1, k, w0:w0 + n_odd, :] = _pack_pair(odd[0:n_odd], even_next)

    def conv_block(rb):
        m0 = rb * CONV_RBW
        for k in range(n_chunks):
            cs = slice(LANES * k, LANES * (k + 1))
            acc = jnp.zeros((2 * CONV_RBW, LANES), BF16)
            for j in range(CONV_WIDTH):
                d = j + HALO - CONV_PAD
                word = pbuf[d % 2, k, m0 + d // 2:m0 + d // 2 + CONV_RBW, :]
                tap = pltpu.bitcast(jnp.broadcast_to(cw_ref[k, j:j + 1, :], (CONV_RBW, LANES)), BF16)
                acc = acc + pltpu.bitcast(word, BF16) * tap
            aw = pltpu.bitcast(acc, jnp.uint32)
            bias = cb_ref[:, cs]
            ybuf[k, pl.ds(2 * m0, CONV_RBW, stride=2), :] = (
                lax.bitcast_convert_type(lax.shift_left(aw, jnp.uint32(16)), F32) + bias)
            ybuf[k, pl.ds(2 * m0 + 1, CONV_RBW, stride=2), :] = (
                lax.bitcast_convert_type(aw & jnp.uint32(HI_MASK), F32) + bias)
            y_ref[2 * m0:2 * (m0 + CONV_RBW), cs] = ybuf[k, 2 * m0:2 * (m0 + CONV_RBW), :].astype(BF16)

    n_blocks = SEQ // 2 // CONV_RBW
    ready = [min(n_blocks, (packed_upto(i) - max_shift - CONV_RBW) // CONV_RBW + 1) for i in range(n_pieces)]
    project(0, [])
    pack(0)
    done = 0
    for i in range(1, n_pieces):
        project(i, list(range(done, ready[i - 1])))
        done = ready[i - 1]
        pack(i)
    for rb in range(done, n_blocks):
        conv_block(rb)


def _glu_conv_call(x, wa, wg, ba, bg, wkv, bkv, cw, cb):
    bn = GLU_BN
    n_chunks = bn // LANES
    col = lambda rows: pl.BlockSpec((rows, bn), lambda b, j: (0, j))
    return pl.pallas_call(
        _glu_conv_kernel,
        grid=(BATCH, CONV_CH // bn),
        in_specs=[pl.BlockSpec((SEQ, D_MODEL), lambda b, j: (b, 0)),
                  col(D_MODEL), col(D_MODEL), col(1), col(1), col(D_MODEL), col(1),
                  pl.BlockSpec((n_chunks, 32, LANES), lambda b, j: (j, 0, 0)),
                  col(1)],
        out_specs=[pl.BlockSpec((SEQ, bn), lambda b, j: (b, j)),
                   pl.BlockSpec((SEQ, bn), lambda b, j: (b, j))],
        out_shape=[jax.ShapeDtypeStruct((TOKENS, CONV_CH), BF16),
                   jax.ShapeDtypeStruct((TOKENS, 4 * KV_CH), BF16)],
        scratch_shapes=[pltpu.VMEM((n_chunks, SEQ + 2 * HALO, LANES), F32),
                        pltpu.VMEM((2, n_chunks, PAIRS, LANES), jnp.uint32),
                        pltpu.VMEM((n_chunks, SEQ, LANES), F32)],
        compiler_params=_cparams("arbitrary", "arbitrary"),
        name="glu_conv",
    )(x, wa, wg, ba, bg, wkv, bkv, cw, cb)


MIX_TS = 512
N_VARIANTS = 3


def _alibi_slope(h):
    return 2.0 ** (-8.0 * (h + 1) / N_Q_HEADS)


def _mixer_kernel(sink_ref,
                  x_ref, yc_ref, kv_ref, wq_ref, bq_ref, wg_ref, bg_ref,
                  clg_ref, clb_ref,
                  wco_ref, wao_ref, wo_ref, l1g_ref, l1b_ref,
                  o_ref,
                  obuf, bias_tab):
    b = pl.program_id(0)
    s = pl.program_id(1)
    ts = MIX_TS
    t0 = pl.multiple_of(s * ts, ts)

    @pl.when((b == 0) & (s == 0))
    def _():
        r = lax.broadcasted_iota(jnp.int32, (BLOCK, SPAN), 0)
        kk = lax.broadcasted_iota(jnp.int32, (BLOCK, SPAN), 1)
        for v in range(N_VARIANTS):
            dist = jnp.abs(kk - r - v * WINDOW)
            inside = dist <= WINDOW
            distf = dist.astype(F32)
            for h in range(N_Q_HEADS):
                bias_tab[v * N_Q_HEADS + h] = jnp.where(inside, -_alibi_slope(h) * distf, NEG_INF)

    xt = x_ref[...]
    xb = xt.astype(BF16)
    q = (jnp.dot(xb, wq_ref[...], preferred_element_type=F32) + bq_ref[...]).astype(BF16)

    yn = _layer_norm(yc_ref[...].astype(F32), clg_ref[...], clb_ref[...])
    act = (yn * jax.nn.sigmoid(yn)).astype(BF16)
    y_conv = jnp.dot(act, wco_ref[...], preferred_element_type=F32)

    low_half = lax.broadcasted_iota(jnp.int32, (BLOCK, LANES), 1) < HEAD_DIM
    zero_q = jnp.zeros((BLOCK, LANES), BF16)
    nt_dims = (((1,), (1,)), ((), ()))
    n_qblocks = ts // BLOCK

    def window_start(qi):
        return pl.multiple_of(jnp.clip(t0 + BLOCK * qi - WINDOW, 0, SEQ - SPAN), BLOCK)

    def scores(qi):
        rows = slice(BLOCK * qi, BLOCK * (qi + 1))
        ws = window_start(qi)
        out = []
        for h in range(N_KV_HEADS):
            kw = kv_ref[pl.ds(ws, SPAN), LANES * h:LANES * (h + 1)]
            qc = [q[rows, 2 * LANES * h:2 * LANES * h + LANES],
                  q[rows, 2 * LANES * h + LANES:2 * LANES * (h + 1)]]
            qst = jnp.concatenate([jnp.where(low_half, qc[0], zero_q), jnp.where(low_half, zero_q, qc[0]),
                                   jnp.where(low_half, qc[1], zero_q), jnp.where(low_half, zero_q, qc[1])], axis=0)
            out.append(lax.dot_general(qst, kw, nt_dims, preferred_element_type=F32))
        return out

    def softmax(qi, sc):
        qs = t0 + BLOCK * qi
        variant = jnp.where(qs == 0, 0, jnp.where(qs == SEQ - BLOCK, 2, 1))
        probs, dens = [], []
        for h in range(N_KV_HEADS):
            ph = []
            for g in range(GQA_GROUP):
                hq = GQA_GROUP * h + g
                sg = sc[h][BLOCK * g:BLOCK * (g + 1), :] + bias_tab[variant * N_Q_HEADS + hq]
                sink = sink_ref[hq]
                m = jnp.maximum(jnp.max(sg, axis=-1, keepdims=True), sink)
                p = jnp.exp(sg - m)
                dens.append(jnp.sum(p, axis=-1, keepdims=True) + jnp.exp(sink - m))
                ph.append(p.astype(BF16))
            probs.append(jnp.concatenate(ph, axis=0))
        return probs, dens

    def weighted_values(qi, probs, dens):
        rows = slice(BLOCK * qi, BLOCK * (qi + 1))
        ws = window_start(qi)
        for h in range(N_KV_HEADS):
            vw = kv_ref[pl.ds(ws, SPAN), 2 * KV_CH + LANES * h:2 * KV_CH + LANES * (h + 1)]
            pv = jnp.dot(probs[h], vw, preferred_element_type=F32)
            og = [pv[BLOCK * g:BLOCK * (g + 1), :] / dens[GQA_GROUP * h + g] for g in range(GQA_GROUP)]
            obuf[rows, 2 * LANES * h:2 * LANES * h + LANES] = jnp.where(low_half, og[0], og[1]).astype(BF16)
            obuf[rows, 2 * LANES * h + LANES:2 * LANES * (h + 1)] = jnp.where(low_half, og[2], og[3]).astype(BF16)

    n_gate_blocks = 2 * D_MODEL // n_qblocks
    gate_blocks = []
    sc_next = scores(0)
    for qi in range(n_qblocks):
        sc_cur = sc_next
        if qi + 1 < n_qblocks:
            sc_next = scores(qi + 1)
        gs = slice(n_gate_blocks * qi, n_gate_blocks * (qi + 1))
        gate_blocks.append(jax.nn.sigmoid(
            jnp.dot(xb, wg_ref[:, gs], preferred_element_type=F32) + bg_ref[:, gs]))
        probs, dens = softmax(qi, sc_cur)
        weighted_values(qi, probs, dens)
    y_attn = jnp.dot(obuf[...], wao_ref[...], preferred_element_type=F32)

    gates = jnp.concatenate(gate_blocks, axis=1)
    merged = gates[:, :D_MODEL] * y_conv + gates[:, D_MODEL:] * y_attn
    out = jnp.dot(merged.astype(BF16), wo_ref[...], preferred_element_type=F32)
    o_ref[...] = _layer_norm(DN_ALPHA * xt + out, l1g_ref[...], l1b_ref[...])


def _mixer_call(sink, x, yc, kv, wq, bq, wg, bg, clg, clb, wco, wao, wo, l1g, l1b):
    ts = MIX_TS
    ns = SEQ // ts
    const = lambda *shape: pl.BlockSpec(shape, lambda b, s, sk: (0,) * len(shape),
                                        pipeline_mode=pl.Buffered(1))
    grid_spec = pltpu.PrefetchScalarGridSpec(
        num_scalar_prefetch=1,
        grid=(BATCH, ns),
        in_specs=[pl.BlockSpec((ts, D_MODEL), lambda b, s, sk: (b * ns + s, 0)),
                  pl.BlockSpec((ts, CONV_CH), lambda b, s, sk: (b * ns + s, 0)),
                  pl.BlockSpec((None, SEQ, 4 * KV_CH), lambda b, s, sk: (b, 0, 0),
                               pipeline_mode=pl.Buffered(1)),
                  const(D_MODEL, D_MODEL), const(1, D_MODEL),
                  const(D_MODEL, 2 * D_MODEL), const(1, 2 * D_MODEL),
                  const(1, CONV_CH), const(1, CONV_CH),
                  const(CONV_CH, D_MODEL), const(D_MODEL, D_MODEL), const(D_MODEL, D_MODEL),
                  const(1, D_MODEL), const(1, D_MODEL)],
        out_specs=pl.BlockSpec((ts, D_MODEL), lambda b, s, sk: (b * ns + s, 0)),
        scratch_shapes=[pltpu.VMEM((ts, D_MODEL), BF16),
                        pltpu.VMEM((N_VARIANTS * N_Q_HEADS, BLOCK, SPAN), F32)],
    )
    return pl.pallas_call(
        _mixer_kernel,
        grid_spec=grid_spec,
        out_shape=jax.ShapeDtypeStruct((TOKENS, D_MODEL), F32),
        compiler_params=_cparams("arbitrary", "arbitrary"),
        name="mixer",
    )(sink, x, yc, kv, wq, bq, wg, bg, clg, clb, wco, wao, wo, l1g, l1b)


ROUTER_ROWS = 32


N_PAIRS = 6
N_CLASSES = N_GROUPS * N_PAIRS
PAIR_LO = (0, 0, 0, 1, 1, 2)
PAIR_HI = (1, 2, 3, 2, 3, 3)
META_ROWS = 8


def _router_kernel(x_ref, w_ref, b_ref, meta_ref, cnt_ref, carry_ref):
    @pl.when(pl.program_id(0) == 0)
    def _():
        carry_ref[...] = jnp.zeros_like(carry_ref)

    x = x_ref[...]
    tm = x.shape[0]
    xh = x.astype(BF16)
    xl = (x - xh.astype(F32)).astype(BF16)
    nt_dims = (((1,), (1,)), ((), ()))
    lt = (lax.dot_general(w_ref[0], xh, nt_dims, preferred_element_type=F32)
          + lax.dot_general(w_ref[1], xh, nt_dims, preferred_element_type=F32)
          + lax.dot_general(w_ref[0], xl, nt_dims, preferred_element_type=F32))
    lt = lt + b_ref[...]
    row4 = lax.broadcasted_iota(jnp.int32, (N_GROUPS, tm), 0)
    gl = lt[0:N_GROUPS, :]
    gmax = jnp.max(gl, axis=0, keepdims=True)
    g_w = 1.0 / jnp.sum(jnp.exp(gl - gmax), axis=0, keepdims=True)
    g_idx = jnp.min(jnp.where(gl == gmax, row4, N_GROUPS), axis=0, keepdims=True)
    e_sel = jnp.zeros((EXPERTS_PER_GROUP, tm), F32)
    for g in range(N_GROUPS):
        lo = N_GROUPS + EXPERTS_PER_GROUP * g
        e_sel = e_sel + jnp.where(g_idx == g, lt[lo:lo + EXPERTS_PER_GROUP, :], 0.0)
    e1 = jnp.max(e_sel, axis=0, keepdims=True)
    i1 = jnp.min(jnp.where(e_sel == e1, row4, EXPERTS_PER_GROUP), axis=0, keepdims=True)
    rest = jnp.where(row4 == i1, -jnp.inf, e_sel)
    e2 = jnp.max(rest, axis=0, keepdims=True)
    i2 = jnp.min(jnp.where(rest == e2, row4, EXPERTS_PER_GROUP), axis=0, keepdims=True)
    t = jnp.exp(e2 - e1)
    w1 = (1.0 / (1.0 + t)) * g_w
    w2 = (t / (1.0 + t)) * g_w
    first_lower = i1 < i2
    e_lo = jnp.minimum(i1, i2)
    e_hi = jnp.maximum(i1, i2)
    w_lo = jnp.where(first_lower, w1, w2)
    w_hi = jnp.where(first_lower, w2, w1)
    pair = jnp.where(e_lo == 0, 0, jnp.where(e_lo == 1, 3, 5)) + e_hi - e_lo - 1
    cls = g_idx * N_PAIRS + pair

    row32 = lax.broadcasted_iota(jnp.int32, (ROUTER_ROWS, tm), 0)
    member = row32 == cls
    onehot = jnp.where(member, 1.0, 0.0)
    src = lax.broadcasted_iota(jnp.int32, (tm, tm), 0)
    dst = lax.broadcasted_iota(jnp.int32, (tm, tm), 1)
    upper = jnp.where(src <= dst, 1.0, 0.0).astype(BF16)
    prefix = jnp.dot(onehot.astype(BF16), upper, preferred_element_type=F32)
    carry = carry_ref[:, 0:1]
    rank = jnp.sum(jnp.where(member, prefix - 1.0 + carry, 0.0), axis=0, keepdims=True)
    carry_ref[...] = carry_ref[...] + jnp.sum(onehot, axis=1, keepdims=True)
    cnt_ref[...] = carry_ref[...]

    meta_ref[...] = jnp.concatenate(
        [cls.astype(F32), rank, w_lo, w_hi, jnp.zeros((META_ROWS - 4, tm), F32)], axis=0)


def _router_call(x, w, b, tm=512):
    return pl.pallas_call(
        _router_kernel,
        grid=(TOKENS // tm,),
        in_specs=[pl.BlockSpec((tm, D_MODEL), lambda i: (i, 0)),
                  pl.BlockSpec((2, ROUTER_ROWS, D_MODEL), lambda i: (0, 0, 0)),
                  pl.BlockSpec((ROUTER_ROWS, tm), lambda i: (0, 0))],
        out_specs=[pl.BlockSpec((META_ROWS, tm), lambda i: (0, i)),
                   pl.BlockSpec((ROUTER_ROWS, LANES), lambda i: (0, 0))],
        out_shape=[jax.ShapeDtypeStruct((META_ROWS, TOKENS), F32),
                   jax.ShapeDtypeStruct((ROUTER_ROWS, LANES), F32)],
        scratch_shapes=[pltpu.VMEM((ROUTER_ROWS, LANES), F32)],
        compiler_params=_cparams("arbitrary"),
        name="router",
    )(x, w, b)


MOE_TM = 256
TOK_ROWS = 8
MOE_ITEMS = TOKENS // MOE_TM + N_CLASSES
SORTED_TILES = MOE_ITEMS + 1
SORTED_TOKENS = SORTED_TILES * MOE_TM
MAX_IDLE_TILES = SORTED_TILES - TOKENS // MOE_TM
DISPATCH_TM = 512
DMA_QUEUES = 2
PLE_BN = 256
DISPATCH_CHUNKS = 4


def _dispatch_tables(meta, cnt):
    cls = meta[0].astype(jnp.int32)
    rank = meta[1].astype(jnp.int32)
    counts = cnt[:N_CLASSES, 0].astype(jnp.int32)
    tiles_per_class = (counts + MOE_TM - 1) // MOE_TM
    tile_end = jnp.cumsum(tiles_per_class)
    starts = (tile_end - tiles_per_class) * MOE_TM
    pos = starts[cls] + rank
    fill_from = starts + counts
    n_tiles = tile_end[-1]
    j = jnp.arange(MOE_ITEMS, dtype=jnp.int32)
    active = j < n_tiles
    jj = jnp.minimum(j, n_tiles - 1)
    c = jnp.minimum(jnp.sum(tile_end[None, :] <= jj[:, None], axis=1), N_CLASSES - 1)
    grp, pair = c // N_PAIRS, c % N_PAIRS
    e_lo = grp * EXPERTS_PER_GROUP + jnp.asarray(PAIR_LO, jnp.int32)[pair]
    e_hi = grp * EXPERTS_PER_GROUP + jnp.asarray(PAIR_HI, jnp.int32)[pair]
    i32 = lambda v: v.astype(jnp.int32)
    fill = jnp.concatenate([i32(fill_from), i32(n_tiles).reshape(1)])
    return (i32(pos).reshape(TOKENS // DISPATCH_TM, 1, DISPATCH_TM), fill,
            i32(jj), i32(e_lo), i32(e_hi), i32(active))


def _dispatch_kernel(fill_ref, pos_ref, x_ref, meta_ref, xs_hbm, stage, zeros, sem, zsem):
    i = pl.program_id(0)
    n = pl.num_programs(0)
    tm = DISPATCH_TM
    slot = i % 2

    def slab_wait(s):
        pltpu.make_async_copy(stage.at[s], xs_hbm.at[pl.ds(0, tm * TOK_ROWS)], sem.at[s]).wait()

    @pl.when(i == 0)
    def _():
        zeros[...] = jnp.zeros_like(zeros)
        for c in range(N_CLASSES):
            dst = pl.multiple_of(fill_ref[c] * TOK_ROWS, TOK_ROWS)
            pltpu.make_async_copy(zeros, xs_hbm.at[pl.ds(dst, MOE_TM * TOK_ROWS)], zsem.at[0]).start()
        for c in range(N_CLASSES):
            pltpu.make_async_copy(zeros, xs_hbm.at[pl.ds(0, MOE_TM * TOK_ROWS)], zsem.at[0]).wait()
        for k in range(MAX_IDLE_TILES):
            t = fill_ref[N_CLASSES] + k

            @pl.when(t < SORTED_TILES)
            def _():
                dst = pl.multiple_of(t * (MOE_TM * TOK_ROWS), MOE_TM * TOK_ROWS)
                tail = pltpu.make_async_copy(zeros, xs_hbm.at[pl.ds(dst, MOE_TM * TOK_ROWS)], zsem.at[0])
                tail.start()
                tail.wait()

    @pl.when(i >= 2)
    def _():
        slab_wait(slot)

    wrows = jnp.concatenate([meta_ref[2:4, :], jnp.zeros((LANES - 2, tm), F32)], axis=0)
    wcols = lax.bitcast_convert_type(wrows.T, jnp.uint32)
    chunk = tm // DISPATCH_CHUNKS
    for c in range(DISPATCH_CHUNKS):
        rows = slice(chunk * c, chunk * (c + 1))
        first = chunk * c * TOK_ROWS
        for g in range(D_MODEL // (2 * LANES)):
            lo = x_ref[rows, 2 * LANES * g:2 * LANES * g + LANES].astype(BF16).astype(F32)
            hi = x_ref[rows, 2 * LANES * g + LANES:2 * LANES * (g + 1)].astype(BF16).astype(F32)
            stage[slot, pl.ds(first + g, chunk, stride=TOK_ROWS), :] = _pack_pair(_bits(lo), _bits(hi))
        stage[slot, pl.ds(first + 4, chunk, stride=TOK_ROWS), :] = wcols[rows, :]
        for g in range(5, TOK_ROWS):
            stage[slot, pl.ds(first + g, chunk, stride=TOK_ROWS), :] = jnp.zeros((chunk, LANES), jnp.uint32)
        for r in range(chunk * c, chunk * (c + 1)):
            dst = pl.multiple_of(pos_ref[0, r] * TOK_ROWS, TOK_ROWS)
            pltpu.make_async_copy(stage.at[slot, pl.ds(r * TOK_ROWS, TOK_ROWS)],
                                  xs_hbm.at[pl.ds(dst, TOK_ROWS)],
                                  sem.at[slot]).start(priority=r % DMA_QUEUES)

    @pl.when(i == n - 1)
    def _():
        slab_wait(1 - slot)
        slab_wait(slot)


def _dispatch_call(tables, x, meta):
    pos, fill_from = tables[0], tables[1]
    tm = DISPATCH_TM
    grid_spec = pltpu.PrefetchScalarGridSpec(
        num_scalar_prefetch=1,
        grid=(TOKENS // tm,),
        in_specs=[pl.BlockSpec((None, 1, tm), lambda i, f: (i, 0, 0), memory_space=pltpu.SMEM),
                  pl.BlockSpec((tm, D_MODEL), lambda i, f: (i, 0)),
                  pl.BlockSpec((META_ROWS, tm), lambda i, f: (0, i))],
        out_specs=pl.BlockSpec(memory_space=pl.ANY),
        scratch_shapes=[pltpu.VMEM((2, tm * TOK_ROWS, LANES), jnp.uint32),
                        pltpu.VMEM((MOE_TM * TOK_ROWS, LANES), jnp.uint32),
                        pltpu.SemaphoreType.DMA((2,)),
                        pltpu.SemaphoreType.DMA((1,))],
    )
    return pl.pallas_call(
        _dispatch_kernel,
        grid_spec=grid_spec,
        out_shape=jax.ShapeDtypeStruct((SORTED_TOKENS * TOK_ROWS, LANES), jnp.uint32),
        compiler_params=_cparams("arbitrary"),
        name="dispatch",
    )(fill_from, pos, x, meta)


def _moe_kernel(in_ref, elo_ref, ehi_ref, active_ref,
                xs_ref, w1a_ref, w3a_ref, w2a_ref, w1b_ref, w3b_ref, w2b_ref, ys_ref):
    j = pl.program_id(0)

    @pl.when(active_ref[j] != 0)
    def _():
        chunks = []
        for g in range(D_MODEL // (2 * LANES)):
            word = xs_ref[pl.ds(g, MOE_TM, stride=TOK_ROWS), :]
            lo = lax.bitcast_convert_type(lax.shift_left(word, jnp.uint32(16)), F32)
            hi = lax.bitcast_convert_type(word & jnp.uint32(HI_MASK), F32)
            chunks += [lo.astype(BF16), hi.astype(BF16)]
        xb = jnp.concatenate(chunks, axis=1)
        wts = lax.bitcast_convert_type(xs_ref[pl.ds(4, MOE_TM, stride=TOK_ROWS), :], F32)
        wa = wts[:, 0:1]
        wb = wts[:, 1:2]
        a1 = jnp.dot(xb, w1a_ref[...], preferred_element_type=F32)
        a3 = jnp.dot(xb, w3a_ref[...], preferred_element_type=F32)
        ha = ((a1 * jax.nn.sigmoid(a1)) * a3 * wa).astype(BF16)
        b1 = jnp.dot(xb, w1b_ref[...], preferred_element_type=F32)
        b3 = jnp.dot(xb, w3b_ref[...], preferred_element_type=F32)
        hb = ((b1 * jax.nn.sigmoid(b1)) * b3 * wb).astype(BF16)
        y = (jnp.dot(ha, w2a_ref[...], preferred_element_type=F32)
             + jnp.dot(hb, w2b_ref[...], preferred_element_type=F32))
        for k in range(N_LANE_CHUNKS):
            ys_ref[pl.ds(k, MOE_TM, stride=TOK_ROWS), :] = y[:, LANES * k:LANES * (k + 1)]

    @pl.when(active_ref[j] == 0)
    def _():
        ys_ref[...] = jnp.zeros_like(ys_ref)


def _moe_call(tables, xs, w1, w3, w2):
    in_tile, e_lo, e_hi, active = tables[2:]
    rows = MOE_TM * TOK_ROWS
    wspec = lambda shape, which: pl.BlockSpec(
        (None,) + shape, lambda j, ti, el, eh, ac: ((el, eh)[which][j], 0, 0))
    grid_spec = pltpu.PrefetchScalarGridSpec(
        num_scalar_prefetch=4,
        grid=(MOE_ITEMS,),
        in_specs=[pl.BlockSpec((rows, LANES), lambda j, ti, el, eh, ac: (ti[j], 0)),
                  wspec((D_MODEL, D_EXPERT), 0), wspec((D_MODEL, D_EXPERT), 0), wspec((D_EXPERT, D_MODEL), 0),
                  wspec((D_MODEL, D_EXPERT), 1), wspec((D_MODEL, D_EXPERT), 1), wspec((D_EXPERT, D_MODEL), 1)],
        out_specs=pl.BlockSpec((rows, LANES), lambda j, ti, el, eh, ac: (j, 0)),
    )
    return pl.pallas_call(
        _moe_kernel,
        grid_spec=grid_spec,
        out_shape=jax.ShapeDtypeStruct((MOE_ITEMS * rows, LANES), F32),
        compiler_params=_cparams("arbitrary"),
        name="moe",
    )(in_tile, e_lo, e_hi, active, xs, w1, w3, w2, w1, w3, w2)


def _ple_ln_kernel(pos_ref, pos_next_ref, x_ref, ys_hbm, p_ref, wpg_ref, bpg_ref, wp_ref, g_ref, b_ref,
                   o_ref, fbuf, sem):
    i = pl.program_id(0)
    n = pl.num_programs(0)
    tm = DISPATCH_TM
    slot = i % 2

    def row_gather(idx_ref, s):
        def body(h, carry):
            for queue in range(DMA_QUEUES):
                r = h * DMA_QUEUES + queue
                src = pl.multiple_of(idx_ref[0, r] * TOK_ROWS, TOK_ROWS)
                dst = pl.multiple_of(r * TOK_ROWS, TOK_ROWS)
                pltpu.make_async_copy(ys_hbm.at[pl.ds(src, TOK_ROWS)], fbuf.at[s, pl.ds(dst, TOK_ROWS)],
                                      sem.at[s]).start(priority=queue)
            return carry
        lax.fori_loop(0, tm // DMA_QUEUES, body, 0, unroll=4)

    def slab_wait(s):
        pltpu.make_async_copy(ys_hbm.at[pl.ds(0, tm * TOK_ROWS)], fbuf.at[s], sem.at[s]).wait()

    @pl.when(i == 0)
    def _():
        row_gather(pos_ref, slot)

    x = x_ref[...]
    xb = x.astype(BF16)
    n_blocks = D_MODEL // PLE_BN
    per_block = tm // n_blocks
    gate_blocks = []
    for c in range(n_blocks):
        for r in range(per_block * c, per_block * (c + 1)):
            src = pl.multiple_of(pos_next_ref[0, r] * TOK_ROWS, TOK_ROWS)
            pltpu.make_async_copy(ys_hbm.at[pl.ds(src, TOK_ROWS)],
                                  fbuf.at[1 - slot, pl.ds(r * TOK_ROWS, TOK_ROWS)],
                                  sem.at[1 - slot]).start(priority=r % DMA_QUEUES)
        cs = slice(PLE_BN * c, PLE_BN * (c + 1))
        gate_blocks.append(jax.nn.sigmoid(
            jnp.dot(xb, wpg_ref[:, cs], preferred_element_type=F32) + bpg_ref[:, cs]))
    gate = jnp.concatenate(gate_blocks, axis=1)
    pe = jnp.dot(p_ref[...].astype(BF16), wp_ref[...], preferred_element_type=F32)
    base = DN_ALPHA * x + gate * pe

    slab_wait(slot)

    @pl.when(i == n - 1)
    def _():
        slab_wait(1 - slot)

    ffn = jnp.concatenate([fbuf[slot, pl.ds(k, tm, stride=TOK_ROWS), :] for k in range(N_LANE_CHUNKS)], axis=1)
    o_ref[...] = _layer_norm(base + ffn, g_ref[...], b_ref[...])


def _ple_ln_call(pos, x, ys, p, wpg, bpg, wp, g, b):
    tm = DISPATCH_TM
    nt = TOKENS // tm
    row = lambda n: pl.BlockSpec((tm, n), lambda i: (i, 0))
    const = lambda *shape: pl.BlockSpec(shape, lambda i: (0,) * len(shape))
    return pl.pallas_call(
        _ple_ln_kernel,
        grid=(nt,),
        in_specs=[pl.BlockSpec((None, 1, tm), lambda i: (i, 0, 0), memory_space=pltpu.SMEM),
                  pl.BlockSpec((None, 1, tm), lambda i: (jnp.minimum(i + 1, nt - 1), 0, 0),
                               memory_space=pltpu.SMEM),
                  row(D_MODEL), pl.BlockSpec(memory_space=pl.ANY), row(D_PLE),
                  const(D_MODEL, D_MODEL), const(1, D_MODEL), const(D_PLE, D_MODEL),
                  const(1, D_MODEL), const(1, D_MODEL)],
        out_specs=row(D_MODEL),
        out_shape=jax.ShapeDtypeStruct((TOKENS, D_MODEL), F32),
        scratch_shapes=[pltpu.VMEM((2, tm * TOK_ROWS, LANES), F32),
                        pltpu.SemaphoreType.DMA((2,))],
        compiler_params=_cparams("arbitrary"),
        name="ple_ln",
    )(pos, pos, x, ys, p, wpg, bpg, wp, g, b)


def _dup_heads(w):
    lead = w.shape[:-1]
    w4 = w.reshape(lead + (N_KV_HEADS, 1, HEAD_DIM))
    return jnp.broadcast_to(w4, lead + (N_KV_HEADS, 2, HEAD_DIM)).reshape(lead + (2 * KV_CH,))


def _row(v):
    return v.reshape(1, -1)


def kernel(x, p, ln_emb_g, ln_emb_b, w_in, b_in, conv_w, conv_b, conv_ln_g, conv_ln_b, w_conv_out, w_attn_out, attn_sink, w_out, ln1_g, ln1_b, w_router_group, b_router_group, w_router_expert, b_router_expert, w1, w3, w2, w_p, w_pg, b_pg, ln2_g, ln2_b):
    c0, c1, c2, c3, c4 = (2 * CONV_CH, 2 * CONV_CH + D_MODEL, 2 * CONV_CH + D_MODEL + KV_CH,
                          2 * CONV_CH + D_MODEL + 2 * KV_CH, 2 * CONV_CH + 2 * D_MODEL + 2 * KV_CH)
    q_scale = HEAD_DIM ** -0.5
    router_tm = 512

    xs = _ln_call(x.reshape(TOKENS, D_MODEL), _row(ln_emb_g), _row(ln_emb_b))
    for i in range(DEPTH):
        w, b = w_in[i], b_in[i]
        wa = w[:, :CONV_CH].astype(BF16)
        wgl = w[:, CONV_CH:c0].astype(BF16)
        wq = (w[:, c0:c1] * q_scale).astype(BF16)
        bq = _row(b[c0:c1] * q_scale)
        wkv = jnp.concatenate([_dup_heads(w[:, c1:c2]), _dup_heads(w[:, c2:c3])], axis=1).astype(BF16)
        bkv = _row(jnp.concatenate([_dup_heads(b[c1:c2]), _dup_heads(b[c2:c3])]))
        wgate = w[:, c3:].astype(BF16)
        bgate = _row(b[c3:])
        cw16 = lax.bitcast_convert_type(jnp.pad(conv_w[i], ((0, 1), (0, 0))).astype(BF16), jnp.uint16)
        cw = cw16.astype(jnp.uint32) * jnp.uint32(0x00010001)
        cw = cw.reshape(32, N_LANE_CHUNKS, LANES).transpose(1, 0, 2)

        yc, kv = _glu_conv_call(xs, wa, wgl, _row(b[:CONV_CH]), _row(b[CONV_CH:c0]), wkv, bkv,
                                cw, _row(conv_b[i]))
        x1 = _mixer_call(attn_sink[i].astype(F32), xs, yc, kv.reshape(BATCH, SEQ, 4 * KV_CH),
                         wq, bq, wgate, bgate, _row(conv_ln_g[i]), _row(conv_ln_b[i]),
                         w_conv_out[i].astype(BF16), w_attn_out[i].astype(BF16), w_out[i].astype(BF16),
                         _row(ln1_g[i]), _row(ln1_b[i]))

        wr = jnp.concatenate([w_router_group[i], w_router_expert[i]], axis=1).T
        wr = jnp.pad(wr, ((0, ROUTER_ROWS - wr.shape[0]), (0, 0)))
        wr_hi = wr.astype(BF16)
        wr_lo = (wr - wr_hi.astype(F32)).astype(BF16)
        br = jnp.pad(jnp.concatenate([b_router_group[i], b_router_expert[i]]), (0, ROUTER_ROWS - 20))
        br = jnp.broadcast_to(br[:, None], (ROUTER_ROWS, router_tm))
        meta, cnt = _router_call(x1, jnp.stack([wr_hi, wr_lo]), br, tm=router_tm)

        tables = _dispatch_tables(meta, cnt)
        sorted_x = _dispatch_call(tables, x1, meta)
        sorted_y = _moe_call(tables, sorted_x,
                             w1[i].reshape(N_EXPERTS, D_MODEL, D_EXPERT).astype(BF16),
                             w3[i].reshape(N_EXPERTS, D_MODEL, D_EXPERT).astype(BF16),
                             w2[i].reshape(N_EXPERTS, D_EXPERT, D_MODEL).astype(BF16))
        xs = _ple_ln_call(tables[0], x1, sorted_y, p[i].reshape(TOKENS, D_PLE), w_pg[i].astype(BF16),
                          _row(b_pg[i]), w_p[i].astype(BF16), _row(ln2_g[i]), _row(ln2_b[i]))
    return xs.reshape(BATCH, SEQ, D_MODEL)
```

```python
import functools

import jax
import jax.numpy as jnp
from jax import lax
from jax.experimental import pallas as pl
from jax.experimental.pallas import tpu as pltpu

D_MODEL = 1024
BATCH = 16
SEQ = 2048
DEPTH = 2
TOKENS = BATCH * SEQ
CONV_CH = D_MODEL
CONV_WIDTH = 31
CONV_PAD = CONV_WIDTH // 2
HEAD_DIM = 64
N_Q_HEADS = 16
N_KV_HEADS = 4
GQA_GROUP = 4
KV_CH = N_KV_HEADS * HEAD_DIM
WINDOW = 128
BLOCK = 128
SPAN = BLOCK + 2 * WINDOW
N_GROUPS = 4
EXPERTS_PER_GROUP = 4
N_EXPERTS = N_GROUPS * EXPERTS_PER_GROUP
D_EXPERT = 512
D_PLE = 256
DN_ALPHA = (2 * DEPTH) ** 0.25
LN_EPS = 1e-5
NEG_INF = -1e30

LANES = 128
N_LANE_CHUNKS = D_MODEL // LANES
VMEM_LIMIT = 56 * 1024 * 1024

F32 = jnp.float32
BF16 = jnp.bfloat16


def _layer_norm(v, g, b):
    mu = jnp.mean(v, axis=-1, keepdims=True)
    d = v - mu
    var = jnp.mean(d * d, axis=-1, keepdims=True)
    return d * lax.rsqrt(var + LN_EPS) * g + b


def _cparams(*sem):
    return pltpu.CompilerParams(dimension_semantics=sem, vmem_limit_bytes=VMEM_LIMIT)


def _ln_kernel(x_ref, g_ref, b_ref, o_ref):
    o_ref[...] = _layer_norm(x_ref[...], g_ref[...], b_ref[...])


def _ln_call(x, g, b, tm=512):
    return pl.pallas_call(
        _ln_kernel,
        grid=(TOKENS // tm,),
        in_specs=[pl.BlockSpec((tm, D_MODEL), lambda i: (i, 0)),
                  pl.BlockSpec((1, D_MODEL), lambda i: (0, 0)),
                  pl.BlockSpec((1, D_MODEL), lambda i: (0, 0))],
        out_specs=pl.BlockSpec((tm, D_MODEL), lambda i: (i, 0)),
        out_shape=jax.ShapeDtypeStruct((TOKENS, D_MODEL), F32),
        compiler_params=_cparams("arbitrary"),
        name="ln_emb",
    )(x, g, b)


GLU_BN = 256
GLU_PIECE = 512
CONV_RBW = 32
HALO = 16
PAIRS = (SEQ + 2 * HALO) // 2
HI_MASK = 0xFFFF0000


def _bits(v):
    return lax.bitcast_convert_type(v, jnp.uint32)


def _pack_pair(lo_bits, hi_bits):
    return lax.shift_right_logical(lo_bits, jnp.uint32(16)) | (hi_bits & jnp.uint32(HI_MASK))


def _glu_conv_kernel(x_ref, w_ref, b_ref, cw_ref, cb_ref, y_ref, kv_ref, ubuf, pbuf, ybuf):
    n_chunks = GLU_BN // LANES
    n_pieces = SEQ // GLU_PIECE
    max_shift = (CONV_WIDTH - 1 + HALO - CONV_PAD) // 2
    for k in range(n_chunks):
        ubuf[k, 0:HALO, :] = jnp.zeros((HALO, LANES), F32)
        ubuf[k, HALO + SEQ:2 * HALO + SEQ, :] = jnp.zeros((HALO, LANES), F32)

    def project(i):
        rows = slice(GLU_PIECE * i, GLU_PIECE * (i + 1))
        acc = jnp.dot(x_ref[rows, :].astype(BF16), w_ref[...], preferred_element_type=F32) + b_ref[...]
        a, g = acc[:, :GLU_BN], acc[:, GLU_BN:2 * GLU_BN]
        kv_ref[rows, :] = acc[:, 2 * GLU_BN:].astype(BF16)
        u = (a * jax.nn.sigmoid(g)).astype(BF16).astype(F32)
        for k in range(n_chunks):
            ubuf[k, HALO + GLU_PIECE * i:HALO + GLU_PIECE * (i + 1), :] = u[:, LANES * k:LANES * (k + 1)]

    def packed_upto(i):
        return PAIRS if i == n_pieces - 1 else GLU_PIECE // 2 * (i + 1)

    def pack(i):
        w0 = 0 if i == 0 else packed_upto(i - 1)
        n = packed_upto(i) - w0
        n_odd = n - 1 if i == n_pieces - 1 else n
        for k in range(n_chunks):
            even = _bits(ubuf[k, pl.ds(2 * w0, n, stride=2), :])
            odd = _bits(ubuf[k, pl.ds(2 * w0 + 1, n, stride=2), :])
            even_next = _bits(ubuf[k, pl.ds(2 * w0 + 2, n_odd, stride=2), :])
            pbuf[0, k, w0:w0 + n, :] = _pack_pair(even, odd)
            pbuf[1, k, w0:w0 + n_odd, :] = _pack_pair(odd[0:n_odd], even_next)

    def conv_block(rb):
        m0 = rb * CONV_RBW
        accs = [[jnp.zeros((2 * CONV_RBW, LANES), BF16) for _ in range(2)] for _ in range(n_chunks)]
        for j in range(CONV_WIDTH):
            d = j + HALO - CONV_PAD
            for k in range(n_chunks):
                word = pbuf[d % 2, k, m0 + d // 2:m0 + d // 2 + CONV_RBW, :]
                tap = pltpu.bitcast(jnp.broadcast_to(cw_ref[k, j:j + 1, :], (CONV_RBW, LANES)), BF16)
                accs[k][d % 2] = accs[k][d % 2] + pltpu.bitcast(word, BF16) * tap
        for k in range(n_chunks):
            cs = slice(LANES * k, LANES * (k + 1))
            aw = pltpu.bitcast(accs[k][0] + accs[k][1], jnp.uint32)
            bias = cb_ref[:, cs]
            ybuf[k, pl.ds(2 * m0, CONV_RBW, stride=2), :] = (
                lax.bitcast_convert_type(lax.shift_left(aw, jnp.uint32(16)), F32) + bias)
            ybuf[k, pl.ds(2 * m0 + 1, CONV_RBW, stride=2), :] = (
                lax.bitcast_convert_type(aw & jnp.uint32(HI_MASK), F32) + bias)
            y_ref[2 * m0:2 * (m0 + CONV_RBW), cs] = ybuf[k, 2 * m0:2 * (m0 + CONV_RBW), :].astype(BF16)

    n_blocks = SEQ // 2 // CONV_RBW
    ready = [min(n_blocks, (packed_upto(i) - max_shift - CONV_RBW) // CONV_RBW + 1) for i in range(n_pieces)]
    project(0)
    pack(0)
    done = 0
    for i in range(1, n_pieces):
        project(i)
        for rb in range(done, ready[i - 1]):
            conv_block(rb)
        done = ready[i - 1]
        pack(i)
    for rb in range(done, n_blocks):
        conv_block(rb)


def _glu_conv_call(x, w, b, cw, cb):
    bn = GLU_BN
    n_chunks = bn // LANES
    return pl.pallas_call(
        _glu_conv_kernel,
        grid=(BATCH, CONV_CH // bn),
        in_specs=[pl.BlockSpec((SEQ, D_MODEL), lambda b, j: (b, 0)),
                  pl.BlockSpec((D_MODEL, 3 * bn), lambda b, j: (0, j)),
                  pl.BlockSpec((1, 3 * bn), lambda b, j: (0, j)),
                  pl.BlockSpec((n_chunks, 32, LANES), lambda b, j: (j, 0, 0)),
                  pl.BlockSpec((1, bn), lambda b, j: (0, j))],
        out_specs=[pl.BlockSpec((SEQ, bn), lambda b, j: (b, j)),
                   pl.BlockSpec((SEQ, bn), lambda b, j: (b, j))],
        out_shape=[jax.ShapeDtypeStruct((TOKENS, CONV_CH), BF16),
                   jax.ShapeDtypeStruct((TOKENS, 4 * KV_CH), BF16)],
        scratch_shapes=[pltpu.VMEM((n_chunks, SEQ + 2 * HALO, LANES), F32),
                        pltpu.VMEM((2, n_chunks, PAIRS, LANES), jnp.uint32),
                        pltpu.VMEM((n_chunks, SEQ, LANES), F32)],
        compiler_params=_cparams("arbitrary", "arbitrary"),
        name="glu_conv",
    )(x, w, b, cw, cb)


MIX_TS = 512
N_VARIANTS = 3


def _alibi_slope(h):
    return 2.0 ** (-8.0 * (h + 1) / N_Q_HEADS)


def _mixer_kernel(sink_ref,
                  x_ref, yc_ref, kv_ref, wq_ref, bq_ref, wg_ref, bg_ref,
                  clg_ref, clb_ref,
                  wco_ref, wao_ref, wo_ref, l1g_ref, l1b_ref,
                  o_ref,
                  obuf, bias_tab):
    b = pl.program_id(0)
    s = pl.program_id(1)
    ts = MIX_TS
    t0 = pl.multiple_of(s * ts, ts)

    @pl.when((b == 0) & (s == 0))
    def _():
        r = lax.broadcasted_iota(jnp.int32, (BLOCK, SPAN), 0)
        kk = lax.broadcasted_iota(jnp.int32, (BLOCK, SPAN), 1)
        for v in range(N_VARIANTS):
            dist = jnp.abs(kk - r - v * WINDOW)
            inside = dist <= WINDOW
            distf = dist.astype(F32)
            for h in range(N_Q_HEADS):
                bias_tab[v * N_Q_HEADS + h] = jnp.where(inside, -_alibi_slope(h) * distf, NEG_INF)

    xt = x_ref[...]
    xb = xt.astype(BF16)
    q = (jnp.dot(xb, wq_ref[...], preferred_element_type=F32) + bq_ref[...]).astype(BF16)

    yn = _layer_norm(yc_ref[...].astype(F32), clg_ref[...], clb_ref[...])
    act = (yn * jax.nn.sigmoid(yn)).astype(BF16)
    y_conv = jnp.dot(act, wco_ref[...], preferred_element_type=F32)

    low_half = lax.broadcasted_iota(jnp.int32, (BLOCK, LANES), 1) < HEAD_DIM
    zero_q = jnp.zeros((BLOCK, LANES), BF16)
    nt_dims = (((1,), (1,)), ((), ()))
    n_qblocks = ts // BLOCK

    def window_start(qi):
        return pl.multiple_of(jnp.clip(t0 + BLOCK * qi - WINDOW, 0, SEQ - SPAN), BLOCK)

    def scores(qi):
        rows = slice(BLOCK * qi, BLOCK * (qi + 1))
        ws = window_start(qi)
        out = []
        for h in range(N_KV_HEADS):
            kw = kv_ref[pl.ds(ws, SPAN), LANES * h:LANES * (h + 1)]
            qc = [q[rows, 2 * LANES * h:2 * LANES * h + LANES],
                  q[rows, 2 * LANES * h + LANES:2 * LANES * (h + 1)]]
            qst = jnp.concatenate([jnp.where(low_half, qc[0], zero_q), jnp.where(low_half, zero_q, qc[0]),
                                   jnp.where(low_half, qc[1], zero_q), jnp.where(low_half, zero_q, qc[1])], axis=0)
            out.append(lax.dot_general(qst, kw, nt_dims, preferred_element_type=F32))
        return out

    def softmax(qi, sc):
        qs = t0 + BLOCK * qi
        variant = jnp.where(qs == 0, 0, jnp.where(qs == SEQ - BLOCK, 2, 1))
        probs, dens = [], []
        for h in range(N_KV_HEADS):
            ph = []
            for g in range(GQA_GROUP):
                hq = GQA_GROUP * h + g
                sg = sc[h][BLOCK * g:BLOCK * (g + 1), :] + bias_tab[variant * N_Q_HEADS + hq]
                sink = sink_ref[hq]
                m = jnp.maximum(jnp.max(sg, axis=-1, keepdims=True), sink)
                p = jnp.exp(sg - m)
                dens.append(jnp.sum(p, axis=-1, keepdims=True) + jnp.exp(sink - m))
                ph.append(p.astype(BF16))
            probs.append(jnp.concatenate(ph, axis=0))
        return probs, dens

    def weighted_values(qi, probs, dens):
        rows = slice(BLOCK * qi, BLOCK * (qi + 1))
        ws = window_start(qi)
        for h in range(N_KV_HEADS):
            vw = kv_ref[pl.ds(ws, SPAN), 2 * KV_CH + LANES * h:2 * KV_CH + LANES * (h + 1)]
            pv = jnp.dot(probs[h], vw, preferred_element_type=F32)
            og = [pv[BLOCK * g:BLOCK * (g + 1), :] / dens[GQA_GROUP * h + g] for g in range(GQA_GROUP)]
            obuf[rows, 2 * LANES * h:2 * LANES * h + LANES] = jnp.where(low_half, og[0], og[1]).astype(BF16)
            obuf[rows, 2 * LANES * h + LANES:2 * LANES * (h + 1)] = jnp.where(low_half, og[2], og[3]).astype(BF16)

    n_gate_blocks = 2 * D_MODEL // n_qblocks
    gate_blocks = []
    sc_next = scores(0)
    for qi in range(n_qblocks):
        sc_cur = sc_next
        if qi + 1 < n_qblocks:
            sc_next = scores(qi + 1)
        gs = slice(n_gate_blocks * qi, n_gate_blocks * (qi + 1))
        gate_blocks.append(jax.nn.sigmoid(
            jnp.dot(xb, wg_ref[:, gs], preferred_element_type=F32) + bg_ref[:, gs]))
        probs, dens = softmax(qi, sc_cur)
        weighted_values(qi, probs, dens)
    y_attn = jnp.dot(obuf[...], wao_ref[...], preferred_element_type=F32)

    gates = jnp.concatenate(gate_blocks, axis=1)
    merged = gates[:, :D_MODEL] * y_conv + gates[:, D_MODEL:] * y_attn
    out = jnp.dot(merged.astype(BF16), wo_ref[...], preferred_element_type=F32)
    o_ref[...] = _layer_norm(DN_ALPHA * xt + out, l1g_ref[...], l1b_ref[...])


def _mixer_call(sink, x, yc, kv, wq, bq, wg, bg, clg, clb, wco, wao, wo, l1g, l1b):
    ts = MIX_TS
    ns = SEQ // ts
    const = lambda *shape: pl.BlockSpec(shape, lambda b, s, sk: (0,) * len(shape),
                                        pipeline_mode=pl.Buffered(1))
    grid_spec = pltpu.PrefetchScalarGridSpec(
        num_scalar_prefetch=1,
        grid=(BATCH, ns),
        in_specs=[pl.BlockSpec((ts, D_MODEL), lambda b, s, sk: (b * ns + s, 0)),
                  pl.BlockSpec((ts, CONV_CH), lambda b, s, sk: (b * ns + s, 0)),
                  pl.BlockSpec((None, SEQ, 4 * KV_CH), lambda b, s, sk: (b, 0, 0),
                               pipeline_mode=pl.Buffered(1)),
                  const(D_MODEL, D_MODEL), const(1, D_MODEL),
                  const(D_MODEL, 2 * D_MODEL), const(1, 2 * D_MODEL),
                  const(1, CONV_CH), const(1, CONV_CH),
                  const(CONV_CH, D_MODEL), const(D_MODEL, D_MODEL), const(D_MODEL, D_MODEL),
                  const(1, D_MODEL), const(1, D_MODEL)],
        out_specs=pl.BlockSpec((ts, D_MODEL), lambda b, s, sk: (b * ns + s, 0)),
        scratch_shapes=[pltpu.VMEM((ts, D_MODEL), BF16),
                        pltpu.VMEM((N_VARIANTS * N_Q_HEADS, BLOCK, SPAN), F32)],
    )
    return pl.pallas_call(
        _mixer_kernel,
        grid_spec=grid_spec,
        out_shape=jax.ShapeDtypeStruct((TOKENS, D_MODEL), F32),
        compiler_params=_cparams("arbitrary", "arbitrary"),
        name="mixer",
    )(sink, x, yc, kv, wq, bq, wg, bg, clg, clb, wco, wao, wo, l1g, l1b)


ROUTER_ROWS = 32


N_PAIRS = 6
N_CLASSES = N_GROUPS * N_PAIRS
PAIR_LO = (0, 0, 0, 1, 1, 2)
PAIR_HI = (1, 2, 3, 2, 3, 3)
META_ROWS = 8


def _router_kernel(x_ref, w_ref, b_ref, meta_ref, cnt_ref, carry_ref):
    @pl.when(pl.program_id(0) == 0)
    def _():
        carry_ref[...] = jnp.zeros_like(carry_ref)

    x = x_ref[...]
    tm = x.shape[0]
    xh = x.astype(BF16)
    xl = (x - xh.astype(F32)).astype(BF16)
    nt_dims = (((1,), (1,)), ((), ()))
    lt = (lax.dot_general(w_ref[0], xh, nt_dims, preferred_element_type=F32)
          + lax.dot_general(w_ref[1], xh, nt_dims, preferred_element_type=F32)
          + lax.dot_general(w_ref[0], xl, nt_dims, preferred_element_type=F32))
    lt = lt + b_ref[...]
    row4 = lax.broadcasted_iota(jnp.int32, (N_GROUPS, tm), 0)
    gl = lt[0:N_GROUPS, :]
    gmax = jnp.max(gl, axis=0, keepdims=True)
    g_w = 1.0 / jnp.sum(jnp.exp(gl - gmax), axis=0, keepdims=True)
    g_idx = jnp.min(jnp.where(gl == gmax, row4, N_GROUPS), axis=0, keepdims=True)
    e_sel = jnp.zeros((EXPERTS_PER_GROUP, tm), F32)
    for g in range(N_GROUPS):
        lo = N_GROUPS + EXPERTS_PER_GROUP * g
        e_sel = e_sel + jnp.where(g_idx == g, lt[lo:lo + EXPERTS_PER_GROUP, :], 0.0)
    e1 = jnp.max(e_sel, axis=0, keepdims=True)
    i1 = jnp.min(jnp.where(e_sel == e1, row4, EXPERTS_PER_GROUP), axis=0, keepdims=True)
    rest = jnp.where(row4 == i1, -jnp.inf, e_sel)
    e2 = jnp.max(rest, axis=0, keepdims=True)
    i2 = jnp.min(jnp.where(rest == e2, row4, EXPERTS_PER_GROUP), axis=0, keepdims=True)
    t = jnp.exp(e2 - e1)
    w1 = (1.0 / (1.0 + t)) * g_w
    w2 = (t / (1.0 + t)) * g_w
    first_lower = i1 < i2
    e_lo = jnp.minimum(i1, i2)
    e_hi = jnp.maximum(i1, i2)
    w_lo = jnp.where(first_lower, w1, w2)
    w_hi = jnp.where(first_lower, w2, w1)
    pair = jnp.where(e_lo == 0, 0, jnp.where(e_lo == 1, 3, 5)) + e_hi - e_lo - 1
    cls = g_idx * N_PAIRS + pair

    row32 = lax.broadcasted_iota(jnp.int32, (ROUTER_ROWS, tm), 0)
    member = row32 == cls
    onehot = jnp.where(member, 1.0, 0.0)
    src = lax.broadcasted_iota(jnp.int32, (tm, tm), 0)
    dst = lax.broadcasted_iota(jnp.int32, (tm, tm), 1)
    upper = jnp.where(src <= dst, 1.0, 0.0).astype(BF16)
    prefix = jnp.dot(onehot.astype(BF16), upper, preferred_element_type=F32)
    carry = carry_ref[:, 0:1]
    rank = jnp.sum(jnp.where(member, prefix - 1.0 + carry, 0.0), axis=0, keepdims=True)
    carry_ref[...] = carry_ref[...] + jnp.sum(onehot, axis=1, keepdims=True)
    cnt_ref[...] = carry_ref[...]

    meta_ref[...] = jnp.concatenate(
        [cls.astype(F32), rank, w_lo, w_hi, jnp.zeros((META_ROWS - 4, tm), F32)], axis=0)


def _router_call(x, w, b, tm=512):
    return pl.pallas_call(
        _router_kernel,
        grid=(TOKENS // tm,),
        in_specs=[pl.BlockSpec((tm, D_MODEL), lambda i: (i, 0)),
                  pl.BlockSpec((2, ROUTER_ROWS, D_MODEL), lambda i: (0, 0, 0)),
                  pl.BlockSpec((ROUTER_ROWS, tm), lambda i: (0, 0))],
        out_specs=[pl.BlockSpec((META_ROWS, tm), lambda i: (0, i)),
                   pl.BlockSpec((ROUTER_ROWS, LANES), lambda i: (0, 0))],
        out_shape=[jax.ShapeDtypeStruct((META_ROWS, TOKENS), F32),
                   jax.ShapeDtypeStruct((ROUTER_ROWS, LANES), F32)],
        scratch_shapes=[pltpu.VMEM((ROUTER_ROWS, LANES), F32)],
        compiler_params=_cparams("arbitrary"),
        name="router",
    )(x, w, b)


MOE_TM = 256
TOK_ROWS = 8
MOE_ITEMS = TOKENS // MOE_TM + N_CLASSES
SORTED_TILES = MOE_ITEMS + 1
SORTED_TOKENS = SORTED_TILES * MOE_TM
MAX_IDLE_TILES = SORTED_TILES - TOKENS // MOE_TM
DISPATCH_TM = 512
FLAG_ACTIVE, FLAG_NEW_EXPERTS = 1, 2
DMA_QUEUES = 2
PLE_BN = 256
DISPATCH_CHUNKS = 4


def _dispatch_tables(meta, cnt):
    cls = meta[0].astype(jnp.int32)
    rank = meta[1].astype(jnp.int32)
    counts = cnt[:N_CLASSES, 0].astype(jnp.int32)
    tiles_per_class = (counts + MOE_TM - 1) // MOE_TM
    tile_end = jnp.cumsum(tiles_per_class)
    starts = (tile_end - tiles_per_class) * MOE_TM
    pos = starts[cls] + rank
    fill_from = starts + counts
    n_tiles = tile_end[-1]
    j = jnp.arange(MOE_ITEMS, dtype=jnp.int32)
    active = j < n_tiles
    jj = jnp.minimum(j, n_tiles - 1)
    c = jnp.minimum(jnp.sum(tile_end[None, :] <= jj[:, None], axis=1), N_CLASSES - 1)
    grp, pair = c // N_PAIRS, c % N_PAIRS
    e_lo = grp * EXPERTS_PER_GROUP + jnp.asarray(PAIR_LO, jnp.int32)[pair]
    e_hi = grp * EXPERTS_PER_GROUP + jnp.asarray(PAIR_HI, jnp.int32)[pair]
    i32 = lambda v: v.astype(jnp.int32)
    fill = jnp.concatenate([i32(fill_from), i32(n_tiles).reshape(1)])
    new_experts = jnp.concatenate([jnp.ones((1,), bool), c[1:] != c[:-1]])
    flags = i32(active) * FLAG_ACTIVE + i32(new_experts) * FLAG_NEW_EXPERTS
    return (i32(pos).reshape(TOKENS // DISPATCH_TM, 1, DISPATCH_TM), fill,
            i32(jj), i32(e_lo), i32(e_hi), flags)


def _dispatch_kernel(fill_ref, pos_ref, x_ref, meta_ref, xs_hbm, stage, zeros, sem, zsem):
    i = pl.program_id(0)
    n = pl.num_programs(0)
    tm = DISPATCH_TM
    slot = i % 2

    def slab_wait(s):
        pltpu.make_async_copy(stage.at[s], xs_hbm.at[pl.ds(0, tm * TOK_ROWS)], sem.at[s]).wait()

    @pl.when(i == 0)
    def _():
        zeros[...] = jnp.zeros_like(zeros)
        for c in range(N_CLASSES):
            dst = pl.multiple_of(fill_ref[c] * TOK_ROWS, TOK_ROWS)
            pltpu.make_async_copy(zeros, xs_hbm.at[pl.ds(dst, MOE_TM * TOK_ROWS)], zsem.at[0]).start()
        for c in range(N_CLASSES):
            pltpu.make_async_copy(zeros, xs_hbm.at[pl.ds(0, MOE_TM * TOK_ROWS)], zsem.at[0]).wait()
        for k in range(MAX_IDLE_TILES):
            t = fill_ref[N_CLASSES] + k

            @pl.when(t < SORTED_TILES)
            def _():
                dst = pl.multiple_of(t * (MOE_TM * TOK_ROWS), MOE_TM * TOK_ROWS)
                tail = pltpu.make_async_copy(zeros, xs_hbm.at[pl.ds(dst, MOE_TM * TOK_ROWS)], zsem.at[0])
                tail.start()
                tail.wait()

    @pl.when(i >= 2)
    def _():
        slab_wait(slot)

    wrows = jnp.concatenate([meta_ref[2:4, :], jnp.zeros((LANES - 2, tm), F32)], axis=0)
    wcols = lax.bitcast_convert_type(wrows.T, jnp.uint32)
    chunk = tm // DISPATCH_CHUNKS
    for c in range(DISPATCH_CHUNKS):
        rows = slice(chunk * c, chunk * (c + 1))
        first = chunk * c * TOK_ROWS
        for g in range(D_MODEL // (2 * LANES)):
            lo = x_ref[rows, 2 * LANES * g:2 * LANES * g + LANES].astype(BF16).astype(F32)
            hi = x_ref[rows, 2 * LANES * g + LANES:2 * LANES * (g + 1)].astype(BF16).astype(F32)
            stage[slot, pl.ds(first + g, chunk, stride=TOK_ROWS), :] = _pack_pair(_bits(lo), _bits(hi))
        stage[slot, pl.ds(first + 4, chunk, stride=TOK_ROWS), :] = wcols[rows, :]
        for g in range(5, TOK_ROWS):
            stage[slot, pl.ds(first + g, chunk, stride=TOK_ROWS), :] = jnp.zeros((chunk, LANES), jnp.uint32)
        for r in range(chunk * c, chunk * (c + 1)):
            dst = pl.multiple_of(pos_ref[0, r] * TOK_ROWS, TOK_ROWS)
            pltpu.make_async_copy(stage.at[slot, pl.ds(r * TOK_ROWS, TOK_ROWS)],
                                  xs_hbm.at[pl.ds(dst, TOK_ROWS)],
                                  sem.at[slot]).start(priority=r % DMA_QUEUES)

    @pl.when(i == n - 1)
    def _():
        slab_wait(1 - slot)
        slab_wait(slot)


def _dispatch_call(tables, x, meta):
    pos, fill_from = tables[0], tables[1]
    tm = DISPATCH_TM
    grid_spec = pltpu.PrefetchScalarGridSpec(
        num_scalar_prefetch=1,
        grid=(TOKENS // tm,),
        in_specs=[pl.BlockSpec((None, 1, tm), lambda i, f: (i, 0, 0), memory_space=pltpu.SMEM),
                  pl.BlockSpec((tm, D_MODEL), lambda i, f: (i, 0)),
                  pl.BlockSpec((META_ROWS, tm), lambda i, f: (0, i))],
        out_specs=pl.BlockSpec(memory_space=pl.ANY),
        scratch_shapes=[pltpu.VMEM((2, tm * TOK_ROWS, LANES), jnp.uint32),
                        pltpu.VMEM((MOE_TM * TOK_ROWS, LANES), jnp.uint32),
                        pltpu.SemaphoreType.DMA((2,)),
                        pltpu.SemaphoreType.DMA((1,))],
    )
    return pl.pallas_call(
        _dispatch_kernel,
        grid_spec=grid_spec,
        out_shape=jax.ShapeDtypeStruct((SORTED_TOKENS * TOK_ROWS, LANES), jnp.uint32),
        compiler_params=_cparams("arbitrary"),
        name="dispatch",
    )(fill_from, pos, x, meta)


def _moe_kernel(in_ref, elo_ref, ehi_ref, flag_ref,
                xs_ref, w1a_ref, w3a_ref, w2a_ref, w1b_ref, w3b_ref, w2b_ref, ys_ref, w_up, w_down):
    j = pl.program_id(0)

    @pl.when((flag_ref[j] & FLAG_NEW_EXPERTS) != 0)
    def _():
        for n, ref in enumerate((w1a_ref, w3a_ref, w1b_ref, w3b_ref)):
            w_up[n] = ref[...].astype(BF16)
        for n, ref in enumerate((w2a_ref, w2b_ref)):
            w_down[n] = ref[...].astype(BF16)

    @pl.when((flag_ref[j] & FLAG_ACTIVE) != 0)
    def _():
        chunks = []
        for g in range(D_MODEL // (2 * LANES)):
            word = xs_ref[pl.ds(g, MOE_TM, stride=TOK_ROWS), :]
            lo = lax.bitcast_convert_type(lax.shift_left(word, jnp.uint32(16)), F32)
            hi = lax.bitcast_convert_type(word & jnp.uint32(HI_MASK), F32)
            chunks += [lo.astype(BF16), hi.astype(BF16)]
        xb = jnp.concatenate(chunks, axis=1)
        wts = lax.bitcast_convert_type(xs_ref[pl.ds(4, MOE_TM, stride=TOK_ROWS), :], F32)
        wa = wts[:, 0:1]
        wb = wts[:, 1:2]
        a1 = jnp.dot(xb, w_up[0], preferred_element_type=F32)
        a3 = jnp.dot(xb, w_up[1], preferred_element_type=F32)
        ha = ((a1 * jax.nn.sigmoid(a1)) * a3 * wa).astype(BF16)
        b1 = jnp.dot(xb, w_up[2], preferred_element_type=F32)
        b3 = jnp.dot(xb, w_up[3], preferred_element_type=F32)
        hb = ((b1 * jax.nn.sigmoid(b1)) * b3 * wb).astype(BF16)
        y = (jnp.dot(ha, w_down[0], preferred_element_type=F32)
             + jnp.dot(hb, w_down[1], preferred_element_type=F32))
        for k in range(N_LANE_CHUNKS):
            ys_ref[pl.ds(k, MOE_TM, stride=TOK_ROWS), :] = y[:, LANES * k:LANES * (k + 1)]

    @pl.when((flag_ref[j] & FLAG_ACTIVE) == 0)
    def _():
        ys_ref[...] = jnp.zeros_like(ys_ref)


def _moe_call(tables, xs, w1, w3, w2, layer):
    in_tile, e_lo, e_hi, flags = tables[2:]
    e_lo = e_lo + layer * N_EXPERTS
    e_hi = e_hi + layer * N_EXPERTS
    rows = MOE_TM * TOK_ROWS
    wspec = lambda shape, which: pl.BlockSpec(
        (None,) + shape, lambda j, ti, el, eh, ac: ((el, eh)[which][j], 0, 0))
    grid_spec = pltpu.PrefetchScalarGridSpec(
        num_scalar_prefetch=4,
        grid=(MOE_ITEMS,),
        in_specs=[pl.BlockSpec((rows, LANES), lambda j, ti, el, eh, ac: (ti[j], 0)),
                  wspec((D_MODEL, D_EXPERT), 0), wspec((D_MODEL, D_EXPERT), 0), wspec((D_EXPERT, D_MODEL), 0),
                  wspec((D_MODEL, D_EXPERT), 1), wspec((D_MODEL, D_EXPERT), 1), wspec((D_EXPERT, D_MODEL), 1)],
        out_specs=pl.BlockSpec((rows, LANES), lambda j, ti, el, eh, ac: (j, 0)),
        scratch_shapes=[pltpu.VMEM((4, D_MODEL, D_EXPERT), BF16),
                        pltpu.VMEM((2, D_EXPERT, D_MODEL), BF16)],
    )
    return pl.pallas_call(
        _moe_kernel,
        grid_spec=grid_spec,
        out_shape=jax.ShapeDtypeStruct((MOE_ITEMS * rows, LANES), F32),
        compiler_params=_cparams("arbitrary"),
        name="moe",
    )(in_tile, e_lo, e_hi, flags, xs, w1, w3, w2, w1, w3, w2)


def _ple_ln_kernel(pos_ref, pos_next_ref, x_ref, ys_hbm, p_ref, wpg_ref, bpg_ref, wp_ref, g_ref, b_ref,
                   o_ref, fbuf, sem):
    i = pl.program_id(0)
    n = pl.num_programs(0)
    tm = DISPATCH_TM
    slot = i % 2

    def row_gather(idx_ref, s):
        def body(h, carry):
            for queue in range(DMA_QUEUES):
                r = h * DMA_QUEUES + queue
                src = pl.multiple_of(idx_ref[0, r] * TOK_ROWS, TOK_ROWS)
                dst = pl.multiple_of(r * TOK_ROWS, TOK_ROWS)
                pltpu.make_async_copy(ys_hbm.at[pl.ds(src, TOK_ROWS)], fbuf.at[s, pl.ds(dst, TOK_ROWS)],
                                      sem.at[s]).start(priority=queue)
            return carry
        lax.fori_loop(0, tm // DMA_QUEUES, body, 0, unroll=4)

    def slab_wait(s):
        pltpu.make_async_copy(ys_hbm.at[pl.ds(0, tm * TOK_ROWS)], fbuf.at[s], sem.at[s]).wait()

    @pl.when(i == 0)
    def _():
        row_gather(pos_ref, slot)

    x = x_ref[...]
    xb = x.astype(BF16)
    n_blocks = D_MODEL // PLE_BN
    per_block = tm // n_blocks
    gate_blocks = []
    for c in range(n_blocks):
        for r in range(per_block * c, per_block * (c + 1)):
            src = pl.multiple_of(pos_next_ref[0, r] * TOK_ROWS, TOK_ROWS)
            pltpu.make_async_copy(ys_hbm.at[pl.ds(src, TOK_ROWS)],
                                  fbuf.at[1 - slot, pl.ds(r * TOK_ROWS, TOK_ROWS)],
                                  sem.at[1 - slot]).start(priority=r % DMA_QUEUES)
        cs = slice(PLE_BN * c, PLE_BN * (c + 1))
        gate_blocks.append(jax.nn.sigmoid(
            jnp.dot(xb, wpg_ref[:, cs], preferred_element_type=F32) + bpg_ref[:, cs]))
    gate = jnp.concatenate(gate_blocks, axis=1)
    pe = jnp.dot(p_ref[...].astype(BF16), wp_ref[...], preferred_element_type=F32)
    base = DN_ALPHA * x + gate * pe

    slab_wait(slot)

    @pl.when(i == n - 1)
    def _():
        slab_wait(1 - slot)

    ffn = jnp.concatenate([fbuf[slot, pl.ds(k, tm, stride=TOK_ROWS), :] for k in range(N_LANE_CHUNKS)], axis=1)
    o_ref[...] = _layer_norm(base + ffn, g_ref[...], b_ref[...])


def _ple_ln_call(pos, x, ys, p, wpg, bpg, wp, g, b):
    tm = DISPATCH_TM
    nt = TOKENS // tm
    row = lambda n: pl.BlockSpec((tm, n), lambda i: (i, 0))
    const = lambda *shape: pl.BlockSpec(shape, lambda i: (0,) * len(shape))
    return pl.pallas_call(
        _ple_ln_kernel,
        grid=(nt,),
        in_specs=[pl.BlockSpec((None, 1, tm), lambda i: (i, 0, 0), memory_space=pltpu.SMEM),
                  pl.BlockSpec((None, 1, tm), lambda i: (jnp.minimum(i + 1, nt - 1), 0, 0),
                               memory_space=pltpu.SMEM),
                  row(D_MODEL), pl.BlockSpec(memory_space=pl.ANY), row(D_PLE),
                  const(D_MODEL, D_MODEL), const(1, D_MODEL), const(D_PLE, D_MODEL),
                  const(1, D_MODEL), const(1, D_MODEL)],
        out_specs=row(D_MODEL),
        out_shape=jax.ShapeDtypeStruct((TOKENS, D_MODEL), F32),
        scratch_shapes=[pltpu.VMEM((2, tm * TOK_ROWS, LANES), F32),
                        pltpu.SemaphoreType.DMA((2,))],
        compiler_params=_cparams("arbitrary"),
        name="ple_ln",
    )(pos, pos, x, ys, p, wpg, bpg, wp, g, b)


def _dup_heads(w):
    lead = w.shape[:-1]
    w4 = w.reshape(lead + (N_KV_HEADS, 1, HEAD_DIM))
    return jnp.broadcast_to(w4, lead + (N_KV_HEADS, 2, HEAD_DIM)).reshape(lead + (2 * KV_CH,))


def _row(v):
    return v.reshape(1, -1)


def kernel(x, p, ln_emb_g, ln_emb_b, w_in, b_in, conv_w, conv_b, conv_ln_g, conv_ln_b, w_conv_out, w_attn_out, attn_sink, w_out, ln1_g, ln1_b, w_router_group, b_router_group, w_router_expert, b_router_expert, w1, w3, w2, w_p, w_pg, b_pg, ln2_g, ln2_b):
    c0, c1, c2, c3, c4 = (2 * CONV_CH, 2 * CONV_CH + D_MODEL, 2 * CONV_CH + D_MODEL + KV_CH,
                          2 * CONV_CH + D_MODEL + 2 * KV_CH, 2 * CONV_CH + 2 * D_MODEL + 2 * KV_CH)
    q_scale = HEAD_DIM ** -0.5
    router_tm = 512

    w1_all = w1.reshape(DEPTH * N_EXPERTS, D_MODEL, D_EXPERT)
    w3_all = w3.reshape(DEPTH * N_EXPERTS, D_MODEL, D_EXPERT)
    w2_all = w2.reshape(DEPTH * N_EXPERTS, D_EXPERT, D_MODEL)
    xs = _ln_call(x.reshape(TOKENS, D_MODEL), _row(ln_emb_g), _row(ln_emb_b))
    for i in range(DEPTH):
        w, b = w_in[i], b_in[i]
        wq = (w[:, c0:c1] * q_scale).astype(BF16)
        bq = _row(b[c0:c1] * q_scale)
        wkv = jnp.concatenate([_dup_heads(w[:, c1:c2]), _dup_heads(w[:, c2:c3])], axis=1)
        bkv = jnp.concatenate([_dup_heads(b[c1:c2]), _dup_heads(b[c2:c3])])
        wgate = w[:, c3:].astype(BF16)
        bgate = _row(b[c3:])
        blocks = lambda m: jnp.stack([m[..., :CONV_CH].reshape(m.shape[:-1] + (-1, GLU_BN)),
                                      m[..., CONV_CH:c0].reshape(m.shape[:-1] + (-1, GLU_BN)),
                                      m[..., c0:].reshape(m.shape[:-1] + (-1, GLU_BN))], axis=-2)
        w_glu = blocks(jnp.concatenate([w[:, :c0], wkv], axis=1)).reshape(D_MODEL, -1).astype(BF16)
        b_glu = blocks(jnp.concatenate([b[:c0], bkv])).reshape(1, -1)
        cw16 = lax.bitcast_convert_type(jnp.pad(conv_w[i], ((0, 1), (0, 0))).astype(BF16), jnp.uint16)
        cw = cw16.astype(jnp.uint32) * jnp.uint32(0x00010001)
        cw = cw.reshape(32, N_LANE_CHUNKS, LANES).transpose(1, 0, 2)

        yc, kv = _glu_conv_call(xs, w_glu, b_glu, cw, _row(conv_b[i]))
        x1 = _mixer_call(attn_sink[i].astype(F32), xs, yc, kv.reshape(BATCH, SEQ, 4 * KV_CH),
                         wq, bq, wgate, bgate, _row(conv_ln_g[i]), _row(conv_ln_b[i]),
                         w_conv_out[i].astype(BF16), w_attn_out[i].astype(BF16), w_out[i].astype(BF16),
                         _row(ln1_g[i]), _row(ln1_b[i]))

        wr = jnp.concatenate([w_router_group[i], w_router_expert[i]], axis=1).T
        wr = jnp.pad(wr, ((0, ROUTER_ROWS - wr.shape[0]), (0, 0)))
        wr_hi = wr.astype(BF16)
        wr_lo = (wr - wr_hi.astype(F32)).astype(BF16)
        br = jnp.pad(jnp.concatenate([b_router_group[i], b_router_expert[i]]), (0, ROUTER_ROWS - 20))
        br = jnp.broadcast_to(br[:, None], (ROUTER_ROWS, router_tm))
        meta, cnt = _router_call(x1, jnp.stack([wr_hi, wr_lo]), br, tm=router_tm)

        tables = _dispatch_tables(meta, cnt)
        sorted_x = _dispatch_call(tables, x1, meta)
        sorted_y = _moe_call(tables, sorted_x, w1_all, w3_all, w2_all, i)
        xs = _ple_ln_call(tables[0], x1, sorted_y, p[i].reshape(TOKENS, D_PLE), w_pg[i].astype(BF16),
                          _row(b_pg[i]), w_p[i].astype(BF16), _row(ln2_g[i]), _row(ln2_b[i]))
    return xs.reshape(BATCH, SEQ, D_MODEL)
```

```python
import functools

import jax
import jax.numpy as jnp
from jax import lax
from jax.experimental import pallas as pl
from jax.experimental.pallas import tpu as pltpu

D_MODEL = 1024
BATCH = 16
SEQ = 2048
DEPTH = 2
TOKENS = BATCH * SEQ
CONV_CH = D_MODEL
CONV_WIDTH = 31
CONV_PAD = CONV_WIDTH // 2
HEAD_DIM = 64
N_Q_HEADS = 16
N_KV_HEADS = 4
GQA_GROUP = 4
KV_CH = N_KV_HEADS * HEAD_DIM
WINDOW = 128
BLOCK = 128
SPAN = BLOCK + 2 * WINDOW
N_GROUPS = 4
EXPERTS_PER_GROUP = 4
N_EXPERTS = N_GROUPS * EXPERTS_PER_GROUP
D_EXPERT = 512
D_PLE = 256
DN_ALPHA = (2 * DEPTH) ** 0.25
LN_EPS = 1e-5
NEG_INF = -1e30
LOG2E = 1.4426950408889634

LANES = 128
N_LANE_CHUNKS = D_MODEL // LANES
VMEM_LIMIT = 56 * 1024 * 1024

F32 = jnp.float32
BF16 = jnp.bfloat16


def _layer_norm(v, g, b):
    mu = jnp.mean(v, axis=-1, keepdims=True)
    d = v - mu
    var = jnp.mean(d * d, axis=-1, keepdims=True)
    return d * lax.rsqrt(var + LN_EPS) * g + b


def _cparams(*sem):
    return pltpu.CompilerParams(dimension_semantics=sem, vmem_limit_bytes=VMEM_LIMIT)


def _ln_kernel(x_ref, g_ref, b_ref, o_ref):
    o_ref[...] = _layer_norm(x_ref[...], g_ref[...], b_ref[...])


def _ln_call(x, g, b, tm=512):
    return pl.pallas_call(
        _ln_kernel,
        grid=(TOKENS // tm,),
        in_specs=[pl.BlockSpec((tm, D_MODEL), lambda i: (i, 0)),
                  pl.BlockSpec((1, D_MODEL), lambda i: (0, 0)),
                  pl.BlockSpec((1, D_MODEL), lambda i: (0, 0))],
        out_specs=pl.BlockSpec((tm, D_MODEL), lambda i: (i, 0)),
        out_shape=jax.ShapeDtypeStruct((TOKENS, D_MODEL), F32),
        compiler_params=_cparams("arbitrary"),
        name="ln_emb",
    )(x, g, b)


GLU_BN = 256
GLU_PIECE = 512
CONV_RBW = 32
HALO = 16
PAIRS = (SEQ + 2 * HALO) // 2
HI_MASK = 0xFFFF0000


def _bits(v):
    return lax.bitcast_convert_type(v, jnp.uint32)


def _pack_pair(lo_bits, hi_bits):
    return lax.shift_right_logical(lo_bits, jnp.uint32(16)) | (hi_bits & jnp.uint32(HI_MASK))


def _glu_conv_kernel(x_ref, w_ref, b_ref, cw_ref, cb_ref, y_ref, kv_ref, ubuf, pbuf, ybuf):
    n_chunks = GLU_BN // LANES
    n_pieces = SEQ // GLU_PIECE
    max_shift = (CONV_WIDTH - 1 + HALO - CONV_PAD) // 2
    for k in range(n_chunks):
        ubuf[k, 0:HALO, :] = jnp.zeros((HALO, LANES), F32)
        ubuf[k, HALO + SEQ:2 * HALO + SEQ, :] = jnp.zeros((HALO, LANES), F32)

    def project(i):
        rows = slice(GLU_PIECE * i, GLU_PIECE * (i + 1))
        acc = jnp.dot(x_ref[rows, :].astype(BF16), w_ref[...], preferred_element_type=F32) + b_ref[...]
        a, g = acc[:, :GLU_BN], acc[:, GLU_BN:2 * GLU_BN]
        kv_ref[rows, :] = acc[:, 2 * GLU_BN:].astype(BF16)
        u = (a * jax.nn.sigmoid(g)).astype(BF16).astype(F32)
        for k in range(n_chunks):
            ubuf[k, HALO + GLU_PIECE * i:HALO + GLU_PIECE * (i + 1), :] = u[:, LANES * k:LANES * (k + 1)]

    def packed_upto(i):
        return PAIRS if i == n_pieces - 1 else GLU_PIECE // 2 * (i + 1)

    def pack(i):
        w0 = 0 if i == 0 else packed_upto(i - 1)
        n = packed_upto(i) - w0
        n_odd = n - 1 if i == n_pieces - 1 else n
        for k in range(n_chunks):
            even = _bits(ubuf[k, pl.ds(2 * w0, n, stride=2), :])
            odd = _bits(ubuf[k, pl.ds(2 * w0 + 1, n, stride=2), :])
            even_next = _bits(ubuf[k, pl.ds(2 * w0 + 2, n_odd, stride=2), :])
            pbuf[0, k, w0:w0 + n, :] = _pack_pair(even, odd)
            pbuf[1, k, w0:w0 + n_odd, :] = _pack_pair(odd[0:n_odd], even_next)

    def conv_block(rb):
        m0 = rb * CONV_RBW
        accs = [[jnp.zeros((2 * CONV_RBW, LANES), BF16) for _ in range(2)] for _ in range(n_chunks)]
        for j in range(CONV_WIDTH):
            d = j + HALO - CONV_PAD
            for k in range(n_chunks):
                word = pbuf[d % 2, k, m0 + d // 2:m0 + d // 2 + CONV_RBW, :]
                tap = pltpu.bitcast(jnp.broadcast_to(cw_ref[k, j:j + 1, :], (CONV_RBW, LANES)), BF16)
                accs[k][d % 2] = accs[k][d % 2] + pltpu.bitcast(word, BF16) * tap
        for k in range(n_chunks):
            cs = slice(LANES * k, LANES * (k + 1))
            aw = pltpu.bitcast(accs[k][0] + accs[k][1], jnp.uint32)
            bias = cb_ref[:, cs]
            ybuf[k, pl.ds(2 * m0, CONV_RBW, stride=2), :] = (
                lax.bitcast_convert_type(lax.shift_left(aw, jnp.uint32(16)), F32) + bias)
            ybuf[k, pl.ds(2 * m0 + 1, CONV_RBW, stride=2), :] = (
                lax.bitcast_convert_type(aw & jnp.uint32(HI_MASK), F32) + bias)
            y_ref[2 * m0:2 * (m0 + CONV_RBW), cs] = ybuf[k, 2 * m0:2 * (m0 + CONV_RBW), :].astype(BF16)

    n_blocks = SEQ // 2 // CONV_RBW
    ready = [min(n_blocks, (packed_upto(i) - max_shift - CONV_RBW) // CONV_RBW + 1) for i in range(n_pieces)]
    project(0)
    pack(0)
    done = 0
    for i in range(1, n_pieces):
        project(i)
        for rb in range(done, ready[i - 1]):
            conv_block(rb)
        done = ready[i - 1]
        pack(i)
    for rb in range(done, n_blocks):
        conv_block(rb)


def _glu_conv_call(x, w, b, cw, cb):
    bn = GLU_BN
    n_chunks = bn // LANES
    return pl.pallas_call(
        _glu_conv_kernel,
        grid=(BATCH, CONV_CH // bn),
        in_specs=[pl.BlockSpec((SEQ, D_MODEL), lambda b, j: (b, 0)),
                  pl.BlockSpec((D_MODEL, 3 * bn), lambda b, j: (0, j)),
                  pl.BlockSpec((1, 3 * bn), lambda b, j: (0, j)),
                  pl.BlockSpec((n_chunks, 32, LANES), lambda b, j: (j, 0, 0)),
                  pl.BlockSpec((1, bn), lambda b, j: (0, j))],
        out_specs=[pl.BlockSpec((SEQ, bn), lambda b, j: (b, j)),
                   pl.BlockSpec((SEQ, bn), lambda b, j: (b, j))],
        out_shape=[jax.ShapeDtypeStruct((TOKENS, CONV_CH), BF16),
                   jax.ShapeDtypeStruct((TOKENS, 4 * KV_CH), BF16)],
        scratch_shapes=[pltpu.VMEM((n_chunks, SEQ + 2 * HALO, LANES), F32),
                        pltpu.VMEM((2, n_chunks, PAIRS, LANES), jnp.uint32),
                        pltpu.VMEM((n_chunks, SEQ, LANES), F32)],
        compiler_params=_cparams("arbitrary", "arbitrary"),
        name="glu_conv",
    )(x, w, b, cw, cb)


MIX_TS = 512
N_VARIANTS = 3


def _alibi_slope(h):
    return 2.0 ** (-8.0 * (h + 1) / N_Q_HEADS)


def _mixer_kernel(sink_ref,
                  x_ref, yc_ref, kv_ref, wq_ref, bq_ref, wg_ref, bg_ref,
                  clg_ref, clb_ref,
                  wco_ref, wao_ref, wo_ref, l1g_ref, l1b_ref,
                  o_ref,
                  obuf, bias_tab):
    b = pl.program_id(0)
    s = pl.program_id(1)
    ts = MIX_TS
    t0 = pl.multiple_of(s * ts, ts)

    @pl.when((b == 0) & (s == 0))
    def _():
        r = lax.broadcasted_iota(jnp.int32, (BLOCK, SPAN), 0)
        kk = lax.broadcasted_iota(jnp.int32, (BLOCK, SPAN), 1)
        for v in range(N_VARIANTS):
            dist = jnp.abs(kk - r - v * WINDOW)
            inside = dist <= WINDOW
            distf = dist.astype(F32)
            for h in range(N_Q_HEADS):
                bias_tab[v * N_Q_HEADS + h] = jnp.where(inside, -(_alibi_slope(h) * LOG2E) * distf, NEG_INF)

    xt = x_ref[...]
    xb = xt.astype(BF16)
    q = (jnp.dot(xb, wq_ref[...], preferred_element_type=F32) + bq_ref[...]).astype(BF16)

    yn = _layer_norm(yc_ref[...].astype(F32), clg_ref[...], clb_ref[...])
    act = (yn * jax.nn.sigmoid(yn)).astype(BF16)
    y_conv = jnp.dot(act, wco_ref[...], preferred_element_type=F32)

    low_half = lax.broadcasted_iota(jnp.int32, (BLOCK, LANES), 1) < HEAD_DIM
    zero_q = jnp.zeros((BLOCK, LANES), BF16)
    nt_dims = (((1,), (1,)), ((), ()))
    n_qblocks = ts // BLOCK

    def window_start(qi):
        return pl.multiple_of(jnp.clip(t0 + BLOCK * qi - WINDOW, 0, SEQ - SPAN), BLOCK)

    def scores(qi):
        rows = slice(BLOCK * qi, BLOCK * (qi + 1))
        ws = window_start(qi)
        out = []
        for h in range(N_KV_HEADS):
            kw = kv_ref[pl.ds(ws, SPAN), LANES * h:LANES * (h + 1)]
            qc = [q[rows, 2 * LANES * h:2 * LANES * h + LANES],
                  q[rows, 2 * LANES * h + LANES:2 * LANES * (h + 1)]]
            qst = jnp.concatenate([jnp.where(low_half, qc[0], zero_q), jnp.where(low_half, zero_q, qc[0]),
                                   jnp.where(low_half, qc[1], zero_q), jnp.where(low_half, zero_q, qc[1])], axis=0)
            out.append(lax.dot_general(qst, kw, nt_dims, preferred_element_type=F32))
        return out

    def softmax(qi, sc):
        qs = t0 + BLOCK * qi
        variant = jnp.where(qs == 0, 0, jnp.where(qs == SEQ - BLOCK, 2, 1))
        probs, dens = [], []
        for h in range(N_KV_HEADS):
            ph = []
            for g in range(GQA_GROUP):
                hq = GQA_GROUP * h + g
                sg = sc[h][BLOCK * g:BLOCK * (g + 1), :] + bias_tab[variant * N_Q_HEADS + hq]
                sink = sink_ref[hq]
                m = jnp.maximum(jnp.max(sg, axis=-1, keepdims=True), sink)
                p = jnp.exp2(sg - m)
                dens.append(jnp.sum(p, axis=-1, keepdims=True) + jnp.exp2(sink - m))
                ph.append(p.astype(BF16))
            probs.append(jnp.concatenate(ph, axis=0))
        return probs, dens

    def weighted_values(qi, probs, dens):
        rows = slice(BLOCK * qi, BLOCK * (qi + 1))
        ws = window_start(qi)
        for h in range(N_KV_HEADS):
            vw = kv_ref[pl.ds(ws, SPAN), 2 * KV_CH + LANES * h:2 * KV_CH + LANES * (h + 1)]
            pv = jnp.dot(probs[h], vw, preferred_element_type=F32)
            og = [pv[BLOCK * g:BLOCK * (g + 1), :] / dens[GQA_GROUP * h + g] for g in range(GQA_GROUP)]
            obuf[rows, 2 * LANES * h:2 * LANES * h + LANES] = jnp.where(low_half, og[0], og[1]).astype(BF16)
            obuf[rows, 2 * LANES * h + LANES:2 * LANES * (h + 1)] = jnp.where(low_half, og[2], og[3]).astype(BF16)

    n_gate_blocks = 2 * D_MODEL // n_qblocks
    gate_blocks = []
    sc_next = scores(0)
    for qi in range(n_qblocks):
        sc_cur = sc_next
        if qi + 1 < n_qblocks:
            sc_next = scores(qi + 1)
        gs = slice(n_gate_blocks * qi, n_gate_blocks * (qi + 1))
        gate_blocks.append(jax.nn.sigmoid(
            jnp.dot(xb, wg_ref[:, gs], preferred_element_type=F32) + bg_ref[:, gs]))
        probs, dens = softmax(qi, sc_cur)
        weighted_values(qi, probs, dens)
    y_attn = jnp.dot(obuf[...], wao_ref[...], preferred_element_type=F32)

    gates = jnp.concatenate(gate_blocks, axis=1)
    merged = gates[:, :D_MODEL] * y_conv + gates[:, D_MODEL:] * y_attn
    out = jnp.dot(merged.astype(BF16), wo_ref[...], preferred_element_type=F32)
    o_ref[...] = _layer_norm(DN_ALPHA * xt + out, l1g_ref[...], l1b_ref[...])


def _mixer_call(sink, x, yc, kv, wq, bq, wg, bg, clg, clb, wco, wao, wo, l1g, l1b):
    ts = MIX_TS
    ns = SEQ // ts
    const = lambda *shape: pl.BlockSpec(shape, lambda b, s, sk: (0,) * len(shape),
                                        pipeline_mode=pl.Buffered(1))
    grid_spec = pltpu.PrefetchScalarGridSpec(
        num_scalar_prefetch=1,
        grid=(BATCH, ns),
        in_specs=[pl.BlockSpec((ts, D_MODEL), lambda b, s, sk: (b * ns + s, 0)),
                  pl.BlockSpec((ts, CONV_CH), lambda b, s, sk: (b * ns + s, 0)),
                  pl.BlockSpec((None, SEQ, 4 * KV_CH), lambda b, s, sk: (b, 0, 0),
                               pipeline_mode=pl.Buffered(1)),
                  const(D_MODEL, D_MODEL), const(1, D_MODEL),
                  const(D_MODEL, 2 * D_MODEL), const(1, 2 * D_MODEL),
                  const(1, CONV_CH), const(1, CONV_CH),
                  const(CONV_CH, D_MODEL), const(D_MODEL, D_MODEL), const(D_MODEL, D_MODEL),
                  const(1, D_MODEL), const(1, D_MODEL)],
        out_specs=pl.BlockSpec((ts, D_MODEL), lambda b, s, sk: (b * ns + s, 0)),
        scratch_shapes=[pltpu.VMEM((ts, D_MODEL), BF16),
                        pltpu.VMEM((N_VARIANTS * N_Q_HEADS, BLOCK, SPAN), F32)],
    )
    return pl.pallas_call(
        _mixer_kernel,
        grid_spec=grid_spec,
        out_shape=jax.ShapeDtypeStruct((TOKENS, D_MODEL), F32),
        compiler_params=_cparams("arbitrary", "arbitrary"),
        name="mixer",
    )(sink, x, yc, kv, wq, bq, wg, bg, clg, clb, wco, wao, wo, l1g, l1b)


ROUTER_ROWS = 32


N_PAIRS = 6
N_CLASSES = N_GROUPS * N_PAIRS
PAIR_LO = (0, 0, 0, 1, 1, 2)
PAIR_HI = (1, 2, 3, 2, 3, 3)
META_ROWS = 8


def _router_kernel(x_ref, w_ref, b_ref, meta_ref, cnt_ref, carry_ref, upper_ref):
    @pl.when(pl.program_id(0) == 0)
    def _():
        carry_ref[...] = jnp.zeros_like(carry_ref)
        src = lax.broadcasted_iota(jnp.int32, upper_ref.shape, 0)
        dst = lax.broadcasted_iota(jnp.int32, upper_ref.shape, 1)
        upper_ref[...] = jnp.where(src <= dst, 1.0, 0.0).astype(BF16)

    x = x_ref[...]
    tm = x.shape[0]
    xh = x.astype(BF16)
    xl = (x - xh.astype(F32)).astype(BF16)
    nt_dims = (((1,), (1,)), ((), ()))
    lt = (lax.dot_general(w_ref[0], xh, nt_dims, preferred_element_type=F32)
          + lax.dot_general(w_ref[1], xh, nt_dims, preferred_element_type=F32)
          + lax.dot_general(w_ref[0], xl, nt_dims, preferred_element_type=F32))
    lt = lt + b_ref[...]
    row4 = lax.broadcasted_iota(jnp.int32, (N_GROUPS, tm), 0)
    gl = lt[0:N_GROUPS, :]
    gmax = jnp.max(gl, axis=0, keepdims=True)
    g_w = 1.0 / jnp.sum(jnp.exp(gl - gmax), axis=0, keepdims=True)
    g_idx = jnp.min(jnp.where(gl == gmax, row4, N_GROUPS), axis=0, keepdims=True)
    e_sel = jnp.zeros((EXPERTS_PER_GROUP, tm), F32)
    for g in range(N_GROUPS):
        lo = N_GROUPS + EXPERTS_PER_GROUP * g
        e_sel = e_sel + jnp.where(g_idx == g, lt[lo:lo + EXPERTS_PER_GROUP, :], 0.0)
    e1 = jnp.max(e_sel, axis=0, keepdims=True)
    i1 = jnp.min(jnp.where(e_sel == e1, row4, EXPERTS_PER_GROUP), axis=0, keepdims=True)
    rest = jnp.where(row4 == i1, -jnp.inf, e_sel)
    e2 = jnp.max(rest, axis=0, keepdims=True)
    i2 = jnp.min(jnp.where(rest == e2, row4, EXPERTS_PER_GROUP), axis=0, keepdims=True)
    t = jnp.exp(e2 - e1)
    w1 = (1.0 / (1.0 + t)) * g_w
    w2 = (t / (1.0 + t)) * g_w
    first_lower = i1 < i2
    e_lo = jnp.minimum(i1, i2)
    e_hi = jnp.maximum(i1, i2)
    w_lo = jnp.where(first_lower, w1, w2)
    w_hi = jnp.where(first_lower, w2, w1)
    pair = jnp.where(e_lo == 0, 0, jnp.where(e_lo == 1, 3, 5)) + e_hi - e_lo - 1
    cls = g_idx * N_PAIRS + pair

    row32 = lax.broadcasted_iota(jnp.int32, (ROUTER_ROWS, tm), 0)
    member = row32 == cls
    onehot = jnp.where(member, 1.0, 0.0)
    prefix = jnp.dot(onehot.astype(BF16), upper_ref[...], preferred_element_type=F32)
    carry = carry_ref[:, 0:1]
    rank = jnp.sum(jnp.where(member, prefix - 1.0 + carry, 0.0), axis=0, keepdims=True)
    carry_ref[...] = carry_ref[...] + jnp.sum(onehot, axis=1, keepdims=True)
    cnt_ref[...] = carry_ref[...]

    meta_ref[...] = jnp.concatenate(
        [cls.astype(F32), rank, w_lo, w_hi, jnp.zeros((META_ROWS - 4, tm), F32)], axis=0)


def _router_call(x, w, b, tm=512):
    return pl.pallas_call(
        _router_kernel,
        grid=(TOKENS // tm,),
        in_specs=[pl.BlockSpec((tm, D_MODEL), lambda i: (i, 0)),
                  pl.BlockSpec((2, ROUTER_ROWS, D_MODEL), lambda i: (0, 0, 0)),
                  pl.BlockSpec((ROUTER_ROWS, tm), lambda i: (0, 0))],
        out_specs=[pl.BlockSpec((META_ROWS, tm), lambda i: (0, i)),
                   pl.BlockSpec((ROUTER_ROWS, LANES), lambda i: (0, 0))],
        out_shape=[jax.ShapeDtypeStruct((META_ROWS, TOKENS), F32),
                   jax.ShapeDtypeStruct((ROUTER_ROWS, LANES), F32)],
        scratch_shapes=[pltpu.VMEM((ROUTER_ROWS, LANES), F32),
                        pltpu.VMEM((tm, tm), BF16)],
        compiler_params=_cparams("arbitrary"),
        name="router",
    )(x, w, b)


MOE_TM = 256
TOK_ROWS = 8
MOE_ITEMS = TOKENS // MOE_TM + N_CLASSES
SORTED_TILES = MOE_ITEMS + 1
SORTED_TOKENS = SORTED_TILES * MOE_TM
MAX_IDLE_TILES = SORTED_TILES - TOKENS // MOE_TM
DISPATCH_TM = 512
FLAG_ACTIVE, FLAG_NEW_EXPERTS = 1, 2
DMA_QUEUES = 2
PLE_BN = 256
DISPATCH_CHUNKS = 4


def _dispatch_tables(meta, cnt):
    cls = meta[0].astype(jnp.int32)
    rank = meta[1].astype(jnp.int32)
    counts = cnt[:N_CLASSES, 0].astype(jnp.int32)
    tiles_per_class = (counts + MOE_TM - 1) // MOE_TM
    tile_end = jnp.cumsum(tiles_per_class)
    starts = (tile_end - tiles_per_class) * MOE_TM
    pos = starts[cls] + rank
    fill_from = starts + counts
    n_tiles = tile_end[-1]
    j = jnp.arange(MOE_ITEMS, dtype=jnp.int32)
    active = j < n_tiles
    jj = jnp.minimum(j, n_tiles - 1)
    c = jnp.minimum(jnp.sum(tile_end[None, :] <= jj[:, None], axis=1), N_CLASSES - 1)
    grp, pair = c // N_PAIRS, c % N_PAIRS
    e_lo = grp * EXPERTS_PER_GROUP + jnp.asarray(PAIR_LO, jnp.int32)[pair]
    e_hi = grp * EXPERTS_PER_GROUP + jnp.asarray(PAIR_HI, jnp.int32)[pair]
    i32 = lambda v: v.astype(jnp.int32)
    fill = jnp.concatenate([i32(fill_from), i32(n_tiles).reshape(1)])
    new_experts = jnp.concatenate([jnp.ones((1,), bool), c[1:] != c[:-1]])
    flags = i32(active) * FLAG_ACTIVE + i32(new_experts) * FLAG_NEW_EXPERTS
    return (i32(pos).reshape(TOKENS // DISPATCH_TM, 1, DISPATCH_TM), fill,
            i32(jj), i32(e_lo), i32(e_hi), flags)


def _dispatch_kernel(fill_ref, pos_ref, x_ref, meta_ref, xs_hbm, stage, zeros, sem, zsem):
    i = pl.program_id(0)
    n = pl.num_programs(0)
    tm = DISPATCH_TM
    slot = i % 2

    def slab_wait(s):
        pltpu.make_async_copy(stage.at[s], xs_hbm.at[pl.ds(0, tm * TOK_ROWS)], sem.at[s]).wait()

    @pl.when(i == 0)
    def _():
        zeros[...] = jnp.zeros_like(zeros)
        for c in range(N_CLASSES):
            dst = pl.multiple_of(fill_ref[c] * TOK_ROWS, TOK_ROWS)
            pltpu.make_async_copy(zeros, xs_hbm.at[pl.ds(dst, MOE_TM * TOK_ROWS)], zsem.at[0]).start()
        for c in range(N_CLASSES):
            pltpu.make_async_copy(zeros, xs_hbm.at[pl.ds(0, MOE_TM * TOK_ROWS)], zsem.at[0]).wait()
        for k in range(MAX_IDLE_TILES):
            t = fill_ref[N_CLASSES] + k

            @pl.when(t < SORTED_TILES)
            def _():
                dst = pl.multiple_of(t * (MOE_TM * TOK_ROWS), MOE_TM * TOK_ROWS)
                tail = pltpu.make_async_copy(zeros, xs_hbm.at[pl.ds(dst, MOE_TM * TOK_ROWS)], zsem.at[0])
                tail.start()
                tail.wait()

    @pl.when(i >= 2)
    def _():
        slab_wait(slot)

    wrows = jnp.concatenate([meta_ref[2:4, :], jnp.zeros((LANES - 2, tm), F32)], axis=0)
    wcols = lax.bitcast_convert_type(wrows.T, jnp.uint32)
    chunk = tm // DISPATCH_CHUNKS
    for c in range(DISPATCH_CHUNKS):
        rows = slice(chunk * c, chunk * (c + 1))
        first = chunk * c * TOK_ROWS
        for g in range(D_MODEL // (2 * LANES)):
            lo = x_ref[rows, 2 * LANES * g:2 * LANES * g + LANES].astype(BF16).astype(F32)
            hi = x_ref[rows, 2 * LANES * g + LANES:2 * LANES * (g + 1)].astype(BF16).astype(F32)
            stage[slot, pl.ds(first + g, chunk, stride=TOK_ROWS), :] = _pack_pair(_bits(lo), _bits(hi))
        stage[slot, pl.ds(first + 4, chunk, stride=TOK_ROWS), :] = wcols[rows, :]
        for g in range(5, TOK_ROWS):
            stage[slot, pl.ds(first + g, chunk, stride=TOK_ROWS), :] = jnp.zeros((chunk, LANES), jnp.uint32)
        for r in range(chunk * c, chunk * (c + 1)):
            dst = pl.multiple_of(pos_ref[0, r] * TOK_ROWS, TOK_ROWS)
            pltpu.make_async_copy(stage.at[slot, pl.ds(r * TOK_ROWS, TOK_ROWS)],
                                  xs_hbm.at[pl.ds(dst, TOK_ROWS)],
                                  sem.at[slot]).start(priority=r % DMA_QUEUES)

    @pl.when(i == n - 1)
    def _():
        slab_wait(1 - slot)
        slab_wait(slot)


def _dispatch_call(tables, x, meta):
    pos, fill_from = tables[0], tables[1]
    tm = DISPATCH_TM
    grid_spec = pltpu.PrefetchScalarGridSpec(
        num_scalar_prefetch=1,
        grid=(TOKENS // tm,),
        in_specs=[pl.BlockSpec((None, 1, tm), lambda i, f: (i, 0, 0), memory_space=pltpu.SMEM),
                  pl.BlockSpec((tm, D_MODEL), lambda i, f: (i, 0)),
                  pl.BlockSpec((META_ROWS, tm), lambda i, f: (0, i))],
        out_specs=pl.BlockSpec(memory_space=pl.ANY),
        scratch_shapes=[pltpu.VMEM((2, tm * TOK_ROWS, LANES), jnp.uint32),
                        pltpu.VMEM((MOE_TM * TOK_ROWS, LANES), jnp.uint32),
                        pltpu.SemaphoreType.DMA((2,)),
                        pltpu.SemaphoreType.DMA((1,))],
    )
    return pl.pallas_call(
        _dispatch_kernel,
        grid_spec=grid_spec,
        out_shape=jax.ShapeDtypeStruct((SORTED_TOKENS * TOK_ROWS, LANES), jnp.uint32),
        compiler_params=_cparams("arbitrary"),
        name="dispatch",
    )(fill_from, pos, x, meta)


def _moe_kernel(in_ref, elo_ref, ehi_ref, flag_ref,
                xs_ref, w1a_ref, w3a_ref, w2a_ref, w1b_ref, w3b_ref, w2b_ref, ys_ref, w_up, w_down):
    j = pl.program_id(0)

    @pl.when((flag_ref[j] & FLAG_NEW_EXPERTS) != 0)
    def _():
        for n, ref in enumerate((w1a_ref, w3a_ref, w1b_ref, w3b_ref)):
            w_up[n] = ref[...].astype(BF16)
        for n, ref in enumerate((w2a_ref, w2b_ref)):
            w_down[n] = ref[...].astype(BF16)

    @pl.when((flag_ref[j] & FLAG_ACTIVE) != 0)
    def _():
        chunks = []
        for g in range(D_MODEL // (2 * LANES)):
            word = xs_ref[pl.ds(g, MOE_TM, stride=TOK_ROWS), :]
            lo = lax.bitcast_convert_type(lax.shift_left(word, jnp.uint32(16)), F32)
            hi = lax.bitcast_convert_type(word & jnp.uint32(HI_MASK), F32)
            chunks += [lo.astype(BF16), hi.astype(BF16)]
        xb = jnp.concatenate(chunks, axis=1)
        wts = lax.bitcast_convert_type(xs_ref[pl.ds(4, MOE_TM, stride=TOK_ROWS), :], F32)
        wa = wts[:, 0:1]
        wb = wts[:, 1:2]
        a1 = jnp.dot(xb, w_up[0], preferred_element_type=F32)
        a3 = jnp.dot(xb, w_up[1], preferred_element_type=F32)
        ha = ((a1 * jax.nn.sigmoid(a1)) * a3 * wa).astype(BF16)
        b1 = jnp.dot(xb, w_up[2], preferred_element_type=F32)
        b3 = jnp.dot(xb, w_up[3], preferred_element_type=F32)
        hb = ((b1 * jax.nn.sigmoid(b1)) * b3 * wb).astype(BF16)
        y = (jnp.dot(ha, w_down[0], preferred_element_type=F32)
             + jnp.dot(hb, w_down[1], preferred_element_type=F32))
        for k in range(N_LANE_CHUNKS):
            ys_ref[pl.ds(k, MOE_TM, stride=TOK_ROWS), :] = y[:, LANES * k:LANES * (k + 1)]

    @pl.when((flag_ref[j] & FLAG_ACTIVE) == 0)
    def _():
        ys_ref[...] = jnp.zeros_like(ys_ref)


def _moe_call(tables, xs, w1, w3, w2, layer):
    in_tile, e_lo, e_hi, flags = tables[2:]
    e_lo = e_lo + layer * N_EXPERTS
    e_hi = e_hi + layer * N_EXPERTS
    rows = MOE_TM * TOK_ROWS
    wspec = lambda shape, which: pl.BlockSpec(
        (None,) + shape, lambda j, ti, el, eh, ac: ((el, eh)[which][j], 0, 0))
    grid_spec = pltpu.PrefetchScalarGridSpec(
        num_scalar_prefetch=4,
        grid=(MOE_ITEMS,),
        in_specs=[pl.BlockSpec((rows, LANES), lambda j, ti, el, eh, ac: (ti[j], 0)),
                  wspec((D_MODEL, D_EXPERT), 0), wspec((D_MODEL, D_EXPERT), 0), wspec((D_EXPERT, D_MODEL), 0),
                  wspec((D_MODEL, D_EXPERT), 1), wspec((D_MODEL, D_EXPERT), 1), wspec((D_EXPERT, D_MODEL), 1)],
        out_specs=pl.BlockSpec((rows, LANES), lambda j, ti, el, eh, ac: (j, 0)),
        scratch_shapes=[pltpu.VMEM((4, D_MODEL, D_EXPERT), BF16),
                        pltpu.VMEM((2, D_EXPERT, D_MODEL), BF16)],
    )
    return pl.pallas_call(
        _moe_kernel,
        grid_spec=grid_spec,
        out_shape=jax.ShapeDtypeStruct((MOE_ITEMS * rows, LANES), F32),
        compiler_params=_cparams("arbitrary"),
        name="moe",
    )(in_tile, e_lo, e_hi, flags, xs, w1, w3, w2, w1, w3, w2)


def _ple_ln_kernel(pos_ref, pos_next_ref, x_ref, ys_hbm, p_ref, wpg_ref, bpg_ref, wp_ref, g_ref, b_ref,
                   o_ref, fbuf, sem):
    i = pl.program_id(0)
    n = pl.num_programs(0)
    tm = DISPATCH_TM
    slot = i % 2

    def row_gather(idx_ref, s):
        def body(h, carry):
            for queue in range(DMA_QUEUES):
                r = h * DMA_QUEUES + queue
                src = pl.multiple_of(idx_ref[0, r] * TOK_ROWS, TOK_ROWS)
                dst = pl.multiple_of(r * TOK_ROWS, TOK_ROWS)
                pltpu.make_async_copy(ys_hbm.at[pl.ds(src, TOK_ROWS)], fbuf.at[s, pl.ds(dst, TOK_ROWS)],
                                      sem.at[s]).start(priority=queue)
            return carry
        lax.fori_loop(0, tm // DMA_QUEUES, body, 0, unroll=4)

    def slab_wait(s):
        pltpu.make_async_copy(ys_hbm.at[pl.ds(0, tm * TOK_ROWS)], fbuf.at[s], sem.at[s]).wait()

    @pl.when(i == 0)
    def _():
        row_gather(pos_ref, slot)

    x = x_ref[...]
    xb = x.astype(BF16)
    n_blocks = D_MODEL // PLE_BN
    per_block = tm // n_blocks
    gate_blocks = []
    for c in range(n_blocks):
        for r in range(per_block * c, per_block * (c + 1)):
            src = pl.multiple_of(pos_next_ref[0, r] * TOK_ROWS, TOK_ROWS)
            pltpu.make_async_copy(ys_hbm.at[pl.ds(src, TOK_ROWS)],
                                  fbuf.at[1 - slot, pl.ds(r * TOK_ROWS, TOK_ROWS)],
                                  sem.at[1 - slot]).start(priority=r % DMA_QUEUES)
        cs = slice(PLE_BN * c, PLE_BN * (c + 1))
        gate_blocks.append(jax.nn.sigmoid(
            jnp.dot(xb, wpg_ref[:, cs], preferred_element_type=F32) + bpg_ref[:, cs]))
    gate = jnp.concatenate(gate_blocks, axis=1)
    pe = jnp.dot(p_ref[...].astype(BF16), wp_ref[...], preferred_element_type=F32)
    base = DN_ALPHA * x + gate * pe

    slab_wait(slot)

    @pl.when(i == n - 1)
    def _():
        slab_wait(1 - slot)

    ffn = jnp.concatenate([fbuf[slot, pl.ds(k, tm, stride=TOK_ROWS), :] for k in range(N_LANE_CHUNKS)], axis=1)
    o_ref[...] = _layer_norm(base + ffn, g_ref[...], b_ref[...])


def _ple_ln_call(pos, x, ys, p, wpg, bpg, wp, g, b):
    tm = DISPATCH_TM
    nt = TOKENS // tm
    row = lambda n: pl.BlockSpec((tm, n), lambda i: (i, 0))
    const = lambda *shape: pl.BlockSpec(shape, lambda i: (0,) * len(shape))
    return pl.pallas_call(
        _ple_ln_kernel,
        grid=(nt,),
        in_specs=[pl.BlockSpec((None, 1, tm), lambda i: (i, 0, 0), memory_space=pltpu.SMEM),
                  pl.BlockSpec((None, 1, tm), lambda i: (jnp.minimum(i + 1, nt - 1), 0, 0),
                               memory_space=pltpu.SMEM),
                  row(D_MODEL), pl.BlockSpec(memory_space=pl.ANY), row(D_PLE),
                  const(D_MODEL, D_MODEL), const(1, D_MODEL), const(D_PLE, D_MODEL),
                  const(1, D_MODEL), const(1, D_MODEL)],
        out_specs=row(D_MODEL),
        out_shape=jax.ShapeDtypeStruct((TOKENS, D_MODEL), F32),
        scratch_shapes=[pltpu.VMEM((2, tm * TOK_ROWS, LANES), F32),
                        pltpu.SemaphoreType.DMA((2,))],
        compiler_params=_cparams("arbitrary"),
        name="ple_ln",
    )(pos, pos, x, ys, p, wpg, bpg, wp, g, b)


def _dup_heads(w):
    lead = w.shape[:-1]
    w4 = w.reshape(lead + (N_KV_HEADS, 1, HEAD_DIM))
    return jnp.broadcast_to(w4, lead + (N_KV_HEADS, 2, HEAD_DIM)).reshape(lead + (2 * KV_CH,))


def _row(v):
    return v.reshape(1, -1)


def kernel(x, p, ln_emb_g, ln_emb_b, w_in, b_in, conv_w, conv_b, conv_ln_g, conv_ln_b, w_conv_out, w_attn_out, attn_sink, w_out, ln1_g, ln1_b, w_router_group, b_router_group, w_router_expert, b_router_expert, w1, w3, w2, w_p, w_pg, b_pg, ln2_g, ln2_b):
    c0, c1, c2, c3, c4 = (2 * CONV_CH, 2 * CONV_CH + D_MODEL, 2 * CONV_CH + D_MODEL + KV_CH,
                          2 * CONV_CH + D_MODEL + 2 * KV_CH, 2 * CONV_CH + 2 * D_MODEL + 2 * KV_CH)
    q_scale = HEAD_DIM ** -0.5 * LOG2E
    router_tm = 512

    w1_all = w1.reshape(DEPTH * N_EXPERTS, D_MODEL, D_EXPERT)
    w3_all = w3.reshape(DEPTH * N_EXPERTS, D_MODEL, D_EXPERT)
    w2_all = w2.reshape(DEPTH * N_EXPERTS, D_EXPERT, D_MODEL)
    xs = _ln_call(x.reshape(TOKENS, D_MODEL), _row(ln_emb_g), _row(ln_emb_b))
    for i in range(DEPTH):
        w, b = w_in[i], b_in[i]
        wq = (w[:, c0:c1] * q_scale).astype(BF16)
        bq = _row(b[c0:c1] * q_scale)
        wkv = jnp.concatenate([_dup_heads(w[:, c1:c2]), _dup_heads(w[:, c2:c3])], axis=1)
        bkv = jnp.concatenate([_dup_heads(b[c1:c2]), _dup_heads(b[c2:c3])])
        wgate = w[:, c3:].astype(BF16)
        bgate = _row(b[c3:])
        blocks = lambda m: jnp.stack([m[..., :CONV_CH].reshape(m.shape[:-1] + (-1, GLU_BN)),
                                      m[..., CONV_CH:c0].reshape(m.shape[:-1] + (-1, GLU_BN)),
                                      m[..., c0:].reshape(m.shape[:-1] + (-1, GLU_BN))], axis=-2)
        w_glu = blocks(jnp.concatenate([w[:, :c0], wkv], axis=1)).reshape(D_MODEL, -1).astype(BF16)
        b_glu = blocks(jnp.concatenate([b[:c0], bkv])).reshape(1, -1)
        cw16 = lax.bitcast_convert_type(jnp.pad(conv_w[i], ((0, 1), (0, 0))).astype(BF16), jnp.uint16)
        cw = cw16.astype(jnp.uint32) * jnp.uint32(0x00010001)
        cw = cw.reshape(32, N_LANE_CHUNKS, LANES).transpose(1, 0, 2)

        yc, kv = _glu_conv_call(xs, w_glu, b_glu, cw, _row(conv_b[i]))
        x1 = _mixer_call(attn_sink[i].astype(F32) * LOG2E, xs, yc, kv.reshape(BATCH, SEQ, 4 * KV_CH),
                         wq, bq, wgate, bgate, _row(conv_ln_g[i]), _row(conv_ln_b[i]),
                         w_conv_out[i].astype(BF16), w_attn_out[i].astype(BF16), w_out[i].astype(BF16),
                         _row(ln1_g[i]), _row(ln1_b[i]))

        wr = jnp.concatenate([w_router_group[i], w_router_expert[i]], axis=1).T
        wr = jnp.pad(wr, ((0, ROUTER_ROWS - wr.shape[0]), (0, 0)))
        wr_hi = wr.astype(BF16)
        wr_lo = (wr - wr_hi.astype(F32)).astype(BF16)
        br = jnp.pad(jnp.concatenate([b_router_group[i], b_router_expert[i]]), (0, ROUTER_ROWS - 20))
        br = jnp.broadcast_to(br[:, None], (ROUTER_ROWS, router_tm))
        meta, cnt = _router_call(x1, jnp.stack([wr_hi, wr_lo]), br, tm=router_tm)

        tables = _dispatch_tables(meta, cnt)
        sorted_x = _dispatch_call(tables, x1, meta)
        sorted_y = _moe_call(tables, sorted_x, w1_all, w3_all, w2_all, i)
        xs = _ple_ln_call(tables[0], x1, sorted_y, p[i].reshape(TOKENS, D_PLE), w_pg[i].astype(BF16),
                          _row(b_pg[i]), w_p[i].astype(BF16), _row(ln2_g[i]), _row(ln2_b[i]))
    return xs.reshape(BATCH, SEQ, D_MODEL)
```

```python
import functools

import jax
import jax.numpy as jnp
from jax import lax
from jax.experimental import pallas as pl
from jax.experimental.pallas import tpu as pltpu

D_MODEL = 1024
BATCH = 16
SEQ = 2048
DEPTH = 2
TOKENS = BATCH * SEQ
CONV_CH = D_MODEL
CONV_WIDTH = 31
CONV_PAD = CONV_WIDTH // 2
HEAD_DIM = 64
N_Q_HEADS = 16
N_KV_HEADS = 4
GQA_GROUP = 4
KV_CH = N_KV_HEADS * HEAD_DIM
WINDOW = 128
BLOCK = 128
SPAN = BLOCK + 2 * WINDOW
N_GROUPS = 4
EXPERTS_PER_GROUP = 4
N_EXPERTS = N_GROUPS * EXPERTS_PER_GROUP
D_EXPERT = 512
D_PLE = 256
DN_ALPHA = (2 * DEPTH) ** 0.25
LN_EPS = 1e-5
NEG_INF = -1e30
LOG2E = 1.4426950408889634

LANES = 128
N_LANE_CHUNKS = D_MODEL // LANES
VMEM_LIMIT = 56 * 1024 * 1024

F32 = jnp.float32
BF16 = jnp.bfloat16


def _layer_norm(v, g, b):
    mu = jnp.mean(v, axis=-1, keepdims=True)
    d = v - mu
    var = jnp.mean(d * d, axis=-1, keepdims=True)
    return d * lax.rsqrt(var + LN_EPS) * g + b


def _sigmoid_from_neg_log2(z):
    return 1.0 / (1.0 + jnp.exp2(z))


def _cparams(*sem):
    return pltpu.CompilerParams(dimension_semantics=sem, vmem_limit_bytes=VMEM_LIMIT)


def _ln_kernel(x_ref, g_ref, b_ref, o_ref):
    o_ref[...] = _layer_norm(x_ref[...], g_ref[...], b_ref[...])


def _ln_call(x, g, b, tm=512):
    return pl.pallas_call(
        _ln_kernel,
        grid=(TOKENS // tm,),
        in_specs=[pl.BlockSpec((tm, D_MODEL), lambda i: (i, 0)),
                  pl.BlockSpec((1, D_MODEL), lambda i: (0, 0)),
                  pl.BlockSpec((1, D_MODEL), lambda i: (0, 0))],
        out_specs=pl.BlockSpec((tm, D_MODEL), lambda i: (i, 0)),
        out_shape=jax.ShapeDtypeStruct((TOKENS, D_MODEL), F32),
        compiler_params=_cparams("arbitrary"),
        name="ln_emb",
    )(x, g, b)


GLU_BN = 256
GLU_PIECE = 512
CONV_RBW = 32
HALO = 16
PAIRS = (SEQ + 2 * HALO) // 2
HI_MASK = 0xFFFF0000


def _bits(v):
    return lax.bitcast_convert_type(v, jnp.uint32)


def _pack_pair(lo_bits, hi_bits):
    return lax.shift_right_logical(lo_bits, jnp.uint32(16)) | (hi_bits & jnp.uint32(HI_MASK))


def _glu_conv_kernel(x_ref, w_ref, b_ref, cw_ref, cb_ref, y_ref, kv_ref, ubuf, pbuf, ybuf):
    n_chunks = GLU_BN // LANES
    n_pieces = SEQ // GLU_PIECE
    max_shift = (CONV_WIDTH - 1 + HALO - CONV_PAD) // 2
    for k in range(n_chunks):
        ubuf[k, 0:HALO, :] = jnp.zeros((HALO, LANES), F32)
        ubuf[k, HALO + SEQ:2 * HALO + SEQ, :] = jnp.zeros((HALO, LANES), F32)

    def project(i):
        rows = slice(GLU_PIECE * i, GLU_PIECE * (i + 1))
        acc = jnp.dot(x_ref[rows, :].astype(BF16), w_ref[...], preferred_element_type=F32) + b_ref[...]
        a, g = acc[:, :GLU_BN], acc[:, GLU_BN:2 * GLU_BN]
        kv_ref[rows, :] = acc[:, 2 * GLU_BN:].astype(BF16)
        u = (a * _sigmoid_from_neg_log2(g)).astype(BF16).astype(F32)
        for k in range(n_chunks):
            ubuf[k, HALO + GLU_PIECE * i:HALO + GLU_PIECE * (i + 1), :] = u[:, LANES * k:LANES * (k + 1)]

    def packed_upto(i):
        return PAIRS if i == n_pieces - 1 else GLU_PIECE // 2 * (i + 1)

    def pack(i):
        w0 = 0 if i == 0 else packed_upto(i - 1)
        n = packed_upto(i) - w0
        n_odd = n - 1 if i == n_pieces - 1 else n
        for k in range(n_chunks):
            even = _bits(ubuf[k, pl.ds(2 * w0, n, stride=2), :])
            odd = _bits(ubuf[k, pl.ds(2 * w0 + 1, n, stride=2), :])
            even_next = _bits(ubuf[k, pl.ds(2 * w0 + 2, n_odd, stride=2), :])
            pbuf[0, k, w0:w0 + n, :] = _pack_pair(even, odd)
            pbuf[1, k, w0:w0 + n_odd, :] = _pack_pair(odd[0:n_odd], even_next)

    def conv_block(rb):
        m0 = rb * CONV_RBW
        accs = [[jnp.zeros((2 * CONV_RBW, LANES), BF16) for _ in range(2)] for _ in range(n_chunks)]
        for j in range(CONV_WIDTH):
            d = j + HALO - CONV_PAD
            for k in range(n_chunks):
                word = pbuf[d % 2, k, m0 + d // 2:m0 + d // 2 + CONV_RBW, :]
                tap = pltpu.bitcast(jnp.broadcast_to(cw_ref[k, j:j + 1, :], (CONV_RBW, LANES)), BF16)
                accs[k][d % 2] = accs[k][d % 2] + pltpu.bitcast(word, BF16) * tap
        for k in range(n_chunks):
            cs = slice(LANES * k, LANES * (k + 1))
            aw = pltpu.bitcast(accs[k][0] + accs[k][1], jnp.uint32)
            bias = cb_ref[:, cs]
            ybuf[k, pl.ds(2 * m0, CONV_RBW, stride=2), :] = (
                lax.bitcast_convert_type(lax.shift_left(aw, jnp.uint32(16)), F32) + bias)
            ybuf[k, pl.ds(2 * m0 + 1, CONV_RBW, stride=2), :] = (
                lax.bitcast_convert_type(aw & jnp.uint32(HI_MASK), F32) + bias)
            y_ref[2 * m0:2 * (m0 + CONV_RBW), cs] = ybuf[k, 2 * m0:2 * (m0 + CONV_RBW), :].astype(BF16)

    n_blocks = SEQ // 2 // CONV_RBW
    ready = [min(n_blocks, (packed_upto(i) - max_shift - CONV_RBW) // CONV_RBW + 1) for i in range(n_pieces)]
    project(0)
    pack(0)
    done = 0
    for i in range(1, n_pieces):
        project(i)
        for rb in range(done, ready[i - 1]):
            conv_block(rb)
        done = ready[i - 1]
        pack(i)
    for rb in range(done, n_blocks):
        conv_block(rb)


def _glu_conv_call(x, w, b, cw, cb):
    bn = GLU_BN
    n_chunks = bn // LANES
    return pl.pallas_call(
        _glu_conv_kernel,
        grid=(BATCH, CONV_CH // bn),
        in_specs=[pl.BlockSpec((SEQ, D_MODEL), lambda b, j: (b, 0)),
                  pl.BlockSpec((D_MODEL, 3 * bn), lambda b, j: (0, j)),
                  pl.BlockSpec((1, 3 * bn), lambda b, j: (0, j)),
                  pl.BlockSpec((n_chunks, 32, LANES), lambda b, j: (j, 0, 0)),
                  pl.BlockSpec((1, bn), lambda b, j: (0, j))],
        out_specs=[pl.BlockSpec((SEQ, bn), lambda b, j: (b, j)),
                   pl.BlockSpec((SEQ, bn), lambda b, j: (b, j))],
        out_shape=[jax.ShapeDtypeStruct((TOKENS, CONV_CH), BF16),
                   jax.ShapeDtypeStruct((TOKENS, 4 * KV_CH), BF16)],
        scratch_shapes=[pltpu.VMEM((n_chunks, SEQ + 2 * HALO, LANES), F32),
                        pltpu.VMEM((2, n_chunks, PAIRS, LANES), jnp.uint32),
                        pltpu.VMEM((n_chunks, SEQ, LANES), F32)],
        compiler_params=_cparams("arbitrary", "arbitrary"),
        name="glu_conv",
    )(x, w, b, cw, cb)


MIX_TS = 512
N_VARIANTS = 3


def _alibi_slope(h):
    return 2.0 ** (-8.0 * (h + 1) / N_Q_HEADS)


def _mixer_kernel(sink_ref,
                  x_ref, yc_ref, kv_ref, wq_ref, bq_ref, wg_ref, bg_ref,
                  clg_ref, clb_ref,
                  wco_ref, wao_ref, wo_ref, l1g_ref, l1b_ref,
                  o_ref,
                  obuf, bias_tab):
    b = pl.program_id(0)
    s = pl.program_id(1)
    ts = MIX_TS
    t0 = pl.multiple_of(s * ts, ts)

    @pl.when((b == 0) & (s == 0))
    def _():
        r = lax.broadcasted_iota(jnp.int32, (BLOCK, SPAN), 0)
        kk = lax.broadcasted_iota(jnp.int32, (BLOCK, SPAN), 1)
        for v in range(N_VARIANTS):
            dist = jnp.abs(kk - r - v * WINDOW)
            inside = dist <= WINDOW
            distf = dist.astype(F32)
            for h in range(N_Q_HEADS):
                bias_tab[v * N_Q_HEADS + h] = jnp.where(inside, -(_alibi_slope(h) * LOG2E) * distf, NEG_INF)

    xt = x_ref[...]
    xb = xt.astype(BF16)
    q = (jnp.dot(xb, wq_ref[...], preferred_element_type=F32) + bq_ref[...]).astype(BF16)

    yn = _layer_norm(yc_ref[...].astype(F32), clg_ref[...], clb_ref[...])
    act = (yn * jax.nn.sigmoid(yn)).astype(BF16)
    y_conv = jnp.dot(act, wco_ref[...], preferred_element_type=F32)

    low_half = lax.broadcasted_iota(jnp.int32, (BLOCK, LANES), 1) < HEAD_DIM
    zero_q = jnp.zeros((BLOCK, LANES), BF16)
    nt_dims = (((1,), (1,)), ((), ()))
    n_qblocks = ts // BLOCK

    def window_start(qi):
        return pl.multiple_of(jnp.clip(t0 + BLOCK * qi - WINDOW, 0, SEQ - SPAN), BLOCK)

    def scores(qi):
        rows = slice(BLOCK * qi, BLOCK * (qi + 1))
        ws = window_start(qi)
        out = []
        for h in range(N_KV_HEADS):
            kw = kv_ref[pl.ds(ws, SPAN), LANES * h:LANES * (h + 1)]
            qc = [q[rows, 2 * LANES * h:2 * LANES * h + LANES],
                  q[rows, 2 * LANES * h + LANES:2 * LANES * (h + 1)]]
            qst = jnp.concatenate([jnp.where(low_half, qc[0], zero_q), jnp.where(low_half, zero_q, qc[0]),
                                   jnp.where(low_half, qc[1], zero_q), jnp.where(low_half, zero_q, qc[1])], axis=0)
            out.append(lax.dot_general(qst, kw, nt_dims, preferred_element_type=F32))
        return out

    def softmax(qi, sc):
        qs = t0 + BLOCK * qi
        variant = jnp.where(qs == 0, 0, jnp.where(qs == SEQ - BLOCK, 2, 1))
        probs, dens = [], []
        for h in range(N_KV_HEADS):
            ph = []
            for g in range(GQA_GROUP):
                hq = GQA_GROUP * h + g
                sg = sc[h][BLOCK * g:BLOCK * (g + 1), :] + bias_tab[variant * N_Q_HEADS + hq]
                sink = sink_ref[hq]
                m = jnp.maximum(jnp.max(sg, axis=-1, keepdims=True), sink)
                p = jnp.exp2(sg - m)
                dens.append(jnp.sum(p, axis=-1, keepdims=True) + jnp.exp2(sink - m))
                ph.append(p.astype(BF16))
            probs.append(jnp.concatenate(ph, axis=0))
        return probs, dens

    def weighted_values(qi, probs, dens):
        rows = slice(BLOCK * qi, BLOCK * (qi + 1))
        ws = window_start(qi)
        for h in range(N_KV_HEADS):
            vw = kv_ref[pl.ds(ws, SPAN), 2 * KV_CH + LANES * h:2 * KV_CH + LANES * (h + 1)]
            pv = jnp.dot(probs[h], vw, preferred_element_type=F32)
            og = [pv[BLOCK * g:BLOCK * (g + 1), :] / dens[GQA_GROUP * h + g] for g in range(GQA_GROUP)]
            obuf[rows, 2 * LANES * h:2 * LANES * h + LANES] = jnp.where(low_half, og[0], og[1]).astype(BF16)
            obuf[rows, 2 * LANES * h + LANES:2 * LANES * (h + 1)] = jnp.where(low_half, og[2], og[3]).astype(BF16)

    n_gate_blocks = 2 * D_MODEL // n_qblocks
    gate_blocks = []
    sc_next = scores(0)
    for qi in range(n_qblocks):
        sc_cur = sc_next
        if qi + 1 < n_qblocks:
            sc_next = scores(qi + 1)
        gs = slice(n_gate_blocks * qi, n_gate_blocks * (qi + 1))
        gate_blocks.append(_sigmoid_from_neg_log2(
            jnp.dot(xb, wg_ref[:, gs], preferred_element_type=F32) + bg_ref[:, gs]))
        probs, dens = softmax(qi, sc_cur)
        weighted_values(qi, probs, dens)
    y_attn = jnp.dot(obuf[...], wao_ref[...], preferred_element_type=F32)

    gates = jnp.concatenate(gate_blocks, axis=1)
    merged = gates[:, :D_MODEL] * y_conv + gates[:, D_MODEL:] * y_attn
    out = jnp.dot(merged.astype(BF16), wo_ref[...], preferred_element_type=F32)
    o_ref[...] = _layer_norm(DN_ALPHA * xt + out, l1g_ref[...], l1b_ref[...])


def _mixer_call(sink, x, yc, kv, wq, bq, wg, bg, clg, clb, wco, wao, wo, l1g, l1b):
    ts = MIX_TS
    ns = SEQ // ts
    const = lambda *shape: pl.BlockSpec(shape, lambda b, s, sk: (0,) * len(shape),
                                        pipeline_mode=pl.Buffered(1))
    grid_spec = pltpu.PrefetchScalarGridSpec(
        num_scalar_prefetch=1,
        grid=(BATCH, ns),
        in_specs=[pl.BlockSpec((ts, D_MODEL), lambda b, s, sk: (b * ns + s, 0)),
                  pl.BlockSpec((ts, CONV_CH), lambda b, s, sk: (b * ns + s, 0)),
                  pl.BlockSpec((None, SEQ, 4 * KV_CH), lambda b, s, sk: (b, 0, 0),
                               pipeline_mode=pl.Buffered(1)),
                  const(D_MODEL, D_MODEL), const(1, D_MODEL),
                  const(D_MODEL, 2 * D_MODEL), const(1, 2 * D_MODEL),
                  const(1, CONV_CH), const(1, CONV_CH),
                  const(CONV_CH, D_MODEL), const(D_MODEL, D_MODEL), const(D_MODEL, D_MODEL),
                  const(1, D_MODEL), const(1, D_MODEL)],
        out_specs=pl.BlockSpec((ts, D_MODEL), lambda b, s, sk: (b * ns + s, 0)),
        scratch_shapes=[pltpu.VMEM((ts, D_MODEL), BF16),
                        pltpu.VMEM((N_VARIANTS * N_Q_HEADS, BLOCK, SPAN), F32)],
    )
    return pl.pallas_call(
        _mixer_kernel,
        grid_spec=grid_spec,
        out_shape=jax.ShapeDtypeStruct((TOKENS, D_MODEL), F32),
        compiler_params=_cparams("arbitrary", "arbitrary"),
        name="mixer",
    )(sink, x, yc, kv, wq, bq, wg, bg, clg, clb, wco, wao, wo, l1g, l1b)


ROUTER_ROWS = 32


N_PAIRS = 6
N_CLASSES = N_GROUPS * N_PAIRS
PAIR_LO = (0, 0, 0, 1, 1, 2)
PAIR_HI = (1, 2, 3, 2, 3, 3)
META_ROWS = 8


def _router_kernel(x_ref, w_ref, b_ref, meta_ref, cnt_ref, carry_ref, upper_ref):
    @pl.when(pl.program_id(0) == 0)
    def _():
        carry_ref[...] = jnp.zeros_like(carry_ref)
        src = lax.broadcasted_iota(jnp.int32, upper_ref.shape, 0)
        dst = lax.broadcasted_iota(jnp.int32, upper_ref.shape, 1)
        upper_ref[...] = jnp.where(src <= dst, 1.0, 0.0).astype(BF16)

    x = x_ref[...]
    tm = x.shape[0]
    xh = x.astype(BF16)
    xl = (x - xh.astype(F32)).astype(BF16)
    nt_dims = (((1,), (1,)), ((), ()))
    lt = (lax.dot_general(w_ref[0], xh, nt_dims, preferred_element_type=F32)
          + lax.dot_general(w_ref[1], xh, nt_dims, preferred_element_type=F32)
          + lax.dot_general(w_ref[0], xl, nt_dims, preferred_element_type=F32))
    lt = lt + b_ref[...]
    row4 = lax.broadcasted_iota(jnp.int32, (N_GROUPS, tm), 0)
    gl = lt[0:N_GROUPS, :]
    gmax = jnp.max(gl, axis=0, keepdims=True)
    g_w = 1.0 / jnp.sum(jnp.exp(gl - gmax), axis=0, keepdims=True)
    g_idx = jnp.min(jnp.where(gl == gmax, row4, N_GROUPS), axis=0, keepdims=True)
    e_sel = jnp.zeros((EXPERTS_PER_GROUP, tm), F32)
    for g in range(N_GROUPS):
        lo = N_GROUPS + EXPERTS_PER_GROUP * g
        e_sel = e_sel + jnp.where(g_idx == g, lt[lo:lo + EXPERTS_PER_GROUP, :], 0.0)
    e1 = jnp.max(e_sel, axis=0, keepdims=True)
    i1 = jnp.min(jnp.where(e_sel == e1, row4, EXPERTS_PER_GROUP), axis=0, keepdims=True)
    rest = jnp.where(row4 == i1, -jnp.inf, e_sel)
    e2 = jnp.max(rest, axis=0, keepdims=True)
    i2 = jnp.min(jnp.where(rest == e2, row4, EXPERTS_PER_GROUP), axis=0, keepdims=True)
    t = jnp.exp(e2 - e1)
    w1 = (1.0 / (1.0 + t)) * g_w
    w2 = (t / (1.0 + t)) * g_w
    first_lower = i1 < i2
    e_lo = jnp.minimum(i1, i2)
    e_hi = jnp.maximum(i1, i2)
    w_lo = jnp.where(first_lower, w1, w2)
    w_hi = jnp.where(first_lower, w2, w1)
    pair = jnp.where(e_lo == 0, 0, jnp.where(e_lo == 1, 3, 5)) + e_hi - e_lo - 1
    cls = g_idx * N_PAIRS + pair

    row32 = lax.broadcasted_iota(jnp.int32, (ROUTER_ROWS, tm), 0)
    member = row32 == cls
    onehot = jnp.where(member, 1.0, 0.0)
    prefix = jnp.dot(onehot.astype(BF16), upper_ref[...], preferred_element_type=F32)
    carry = carry_ref[:, 0:1]
    rank = jnp.sum(jnp.where(member, prefix - 1.0 + carry, 0.0), axis=0, keepdims=True)
    carry_ref[...] = carry_ref[...] + jnp.sum(onehot, axis=1, keepdims=True)
    cnt_ref[...] = carry_ref[...]

    meta_ref[...] = jnp.concatenate(
        [cls.astype(F32), rank, w_lo, w_hi, jnp.zeros((META_ROWS - 4, tm), F32)], axis=0)


def _router_call(x, w, b, tm=512):
    return pl.pallas_call(
        _router_kernel,
        grid=(TOKENS // tm,),
        in_specs=[pl.BlockSpec((tm, D_MODEL), lambda i: (i, 0)),
                  pl.BlockSpec((2, ROUTER_ROWS, D_MODEL), lambda i: (0, 0, 0)),
                  pl.BlockSpec((ROUTER_ROWS, tm), lambda i: (0, 0))],
        out_specs=[pl.BlockSpec((META_ROWS, tm), lambda i: (0, i)),
                   pl.BlockSpec((ROUTER_ROWS, LANES), lambda i: (0, 0))],
        out_shape=[jax.ShapeDtypeStruct((META_ROWS, TOKENS), F32),
                   jax.ShapeDtypeStruct((ROUTER_ROWS, LANES), F32)],
        scratch_shapes=[pltpu.VMEM((ROUTER_ROWS, LANES), F32),
                        pltpu.VMEM((tm, tm), BF16)],
        compiler_params=_cparams("arbitrary"),
        name="router",
    )(x, w, b)


MOE_TM = 256
TOK_ROWS = 8
MOE_ITEMS = TOKENS // MOE_TM + N_CLASSES
SORTED_TILES = MOE_ITEMS + 1
SORTED_TOKENS = SORTED_TILES * MOE_TM
MAX_IDLE_TILES = SORTED_TILES - TOKENS // MOE_TM
DISPATCH_TM = 512
FLAG_ACTIVE, FLAG_NEW_EXPERTS = 1, 2
DMA_QUEUES = 2
PLE_BN = 256
DISPATCH_CHUNKS = 4


def _dispatch_tables(meta, cnt):
    cls = meta[0].astype(jnp.int32)
    rank = meta[1].astype(jnp.int32)
    counts = cnt[:N_CLASSES, 0].astype(jnp.int32)
    tiles_per_class = (counts + MOE_TM - 1) // MOE_TM
    tile_end = jnp.cumsum(tiles_per_class)
    starts = (tile_end - tiles_per_class) * MOE_TM
    pos = starts[cls] + rank
    fill_from = starts + counts
    n_tiles = tile_end[-1]
    j = jnp.arange(MOE_ITEMS, dtype=jnp.int32)
    active = j < n_tiles
    jj = jnp.minimum(j, n_tiles - 1)
    c = jnp.minimum(jnp.sum(tile_end[None, :] <= jj[:, None], axis=1), N_CLASSES - 1)
    grp, pair = c // N_PAIRS, c % N_PAIRS
    e_lo = grp * EXPERTS_PER_GROUP + jnp.asarray(PAIR_LO, jnp.int32)[pair]
    e_hi = grp * EXPERTS_PER_GROUP + jnp.asarray(PAIR_HI, jnp.int32)[pair]
    i32 = lambda v: v.astype(jnp.int32)
    fill = jnp.concatenate([i32(fill_from), i32(n_tiles).reshape(1)])
    new_experts = jnp.concatenate([jnp.ones((1,), bool), c[1:] != c[:-1]])
    flags = i32(active) * FLAG_ACTIVE + i32(new_experts) * FLAG_NEW_EXPERTS
    return (i32(pos).reshape(TOKENS // DISPATCH_TM, 1, DISPATCH_TM), fill,
            i32(jj), i32(e_lo), i32(e_hi), flags)


def _dispatch_kernel(fill_ref, pos_ref, x_ref, meta_ref, xs_hbm, stage, zeros, sem, zsem):
    i = pl.program_id(0)
    n = pl.num_programs(0)
    tm = DISPATCH_TM
    slot = i % 2

    def slab_wait(s):
        pltpu.make_async_copy(stage.at[s], xs_hbm.at[pl.ds(0, tm * TOK_ROWS)], sem.at[s]).wait()

    @pl.when(i == 0)
    def _():
        zeros[...] = jnp.zeros_like(zeros)
        for c in range(N_CLASSES):
            dst = pl.multiple_of(fill_ref[c] * TOK_ROWS, TOK_ROWS)
            pltpu.make_async_copy(zeros, xs_hbm.at[pl.ds(dst, MOE_TM * TOK_ROWS)], zsem.at[0]).start()
        for c in range(N_CLASSES):
            pltpu.make_async_copy(zeros, xs_hbm.at[pl.ds(0, MOE_TM * TOK_ROWS)], zsem.at[0]).wait()
        for k in range(MAX_IDLE_TILES):
            t = fill_ref[N_CLASSES] + k

            @pl.when(t < SORTED_TILES)
            def _():
                dst = pl.multiple_of(t * (MOE_TM * TOK_ROWS), MOE_TM * TOK_ROWS)
                tail = pltpu.make_async_copy(zeros, xs_hbm.at[pl.ds(dst, MOE_TM * TOK_ROWS)], zsem.at[0])
                tail.start()
                tail.wait()

    @pl.when(i >= 2)
    def _():
        slab_wait(slot)

    wrows = jnp.concatenate([meta_ref[2:4, :], jnp.zeros((LANES - 2, tm), F32)], axis=0)
    wcols = lax.bitcast_convert_type(wrows.T, jnp.uint32)
    chunk = tm // DISPATCH_CHUNKS
    for c in range(DISPATCH_CHUNKS):
        rows = slice(chunk * c, chunk * (c + 1))
        first = chunk * c * TOK_ROWS
        for g in range(D_MODEL // (2 * LANES)):
            lo = x_ref[rows, 2 * LANES * g:2 * LANES * g + LANES].astype(BF16).astype(F32)
            hi = x_ref[rows, 2 * LANES * g + LANES:2 * LANES * (g + 1)].astype(BF16).astype(F32)
            stage[slot, pl.ds(first + g, chunk, stride=TOK_ROWS), :] = _pack_pair(_bits(lo), _bits(hi))
        stage[slot, pl.ds(first + 4, chunk, stride=TOK_ROWS), :] = wcols[rows, :]
        for g in range(5, TOK_ROWS):
            stage[slot, pl.ds(first + g, chunk, stride=TOK_ROWS), :] = jnp.zeros((chunk, LANES), jnp.uint32)
        for r in range(chunk * c, chunk * (c + 1)):
            dst = pl.multiple_of(pos_ref[0, r] * TOK_ROWS, TOK_ROWS)
            pltpu.make_async_copy(stage.at[slot, pl.ds(r * TOK_ROWS, TOK_ROWS)],
                                  xs_hbm.at[pl.ds(dst, TOK_ROWS)],
                                  sem.at[slot]).start(priority=r % DMA_QUEUES)

    @pl.when(i == n - 1)
    def _():
        slab_wait(1 - slot)
        slab_wait(slot)


def _dispatch_call(tables, x, meta):
    pos, fill_from = tables[0], tables[1]
    tm = DISPATCH_TM
    grid_spec = pltpu.PrefetchScalarGridSpec(
        num_scalar_prefetch=1,
        grid=(TOKENS // tm,),
        in_specs=[pl.BlockSpec((None, 1, tm), lambda i, f: (i, 0, 0), memory_space=pltpu.SMEM),
                  pl.BlockSpec((tm, D_MODEL), lambda i, f: (i, 0)),
                  pl.BlockSpec((META_ROWS, tm), lambda i, f: (0, i))],
        out_specs=pl.BlockSpec(memory_space=pl.ANY),
        scratch_shapes=[pltpu.VMEM((2, tm * TOK_ROWS, LANES), jnp.uint32),
                        pltpu.VMEM((MOE_TM * TOK_ROWS, LANES), jnp.uint32),
                        pltpu.SemaphoreType.DMA((2,)),
                        pltpu.SemaphoreType.DMA((1,))],
    )
    return pl.pallas_call(
        _dispatch_kernel,
        grid_spec=grid_spec,
        out_shape=jax.ShapeDtypeStruct((SORTED_TOKENS * TOK_ROWS, LANES), jnp.uint32),
        compiler_params=_cparams("arbitrary"),
        name="dispatch",
    )(fill_from, pos, x, meta)


def _moe_kernel(in_ref, elo_ref, ehi_ref, flag_ref,
                xs_ref, w1a_ref, w3a_ref, w2a_ref, w1b_ref, w3b_ref, w2b_ref, ys_ref, w_up, w_down):
    j = pl.program_id(0)

    @pl.when((flag_ref[j] & FLAG_NEW_EXPERTS) != 0)
    def _():
        for n, ref in enumerate((w1a_ref, w3a_ref, w1b_ref, w3b_ref)):
            w_up[n] = ref[...].astype(BF16)
        for n, ref in enumerate((w2a_ref, w2b_ref)):
            w_down[n] = ref[...].astype(BF16)

    @pl.when((flag_ref[j] & FLAG_ACTIVE) != 0)
    def _():
        chunks = []
        for g in range(D_MODEL // (2 * LANES)):
            word = xs_ref[pl.ds(g, MOE_TM, stride=TOK_ROWS), :]
            lo = lax.bitcast_convert_type(lax.shift_left(word, jnp.uint32(16)), F32)
            hi = lax.bitcast_convert_type(word & jnp.uint32(HI_MASK), F32)
            chunks += [lo.astype(BF16), hi.astype(BF16)]
        xb = jnp.concatenate(chunks, axis=1)
        wts = lax.bitcast_convert_type(xs_ref[pl.ds(4, MOE_TM, stride=TOK_ROWS), :], F32)
        wa = wts[:, 0:1]
        wb = wts[:, 1:2]
        a1 = jnp.dot(xb, w_up[0], preferred_element_type=F32)
        a3 = jnp.dot(xb, w_up[1], preferred_element_type=F32)
        ha = ((a1 * jax.nn.sigmoid(a1)) * a3 * wa).astype(BF16)
        b1 = jnp.dot(xb, w_up[2], preferred_element_type=F32)
        b3 = jnp.dot(xb, w_up[3], preferred_element_type=F32)
        hb = ((b1 * jax.nn.sigmoid(b1)) * b3 * wb).astype(BF16)
        y = (jnp.dot(ha, w_down[0], preferred_element_type=F32)
             + jnp.dot(hb, w_down[1], preferred_element_type=F32))
        for k in range(N_LANE_CHUNKS):
            ys_ref[pl.ds(k, MOE_TM, stride=TOK_ROWS), :] = y[:, LANES * k:LANES * (k + 1)]

    @pl.when((flag_ref[j] & FLAG_ACTIVE) == 0)
    def _():
        ys_ref[...] = jnp.zeros_like(ys_ref)


def _moe_call(tables, xs, w1, w3, w2, layer):
    in_tile, e_lo, e_hi, flags = tables[2:]
    e_lo = e_lo + layer * N_EXPERTS
    e_hi = e_hi + layer * N_EXPERTS
    rows = MOE_TM * TOK_ROWS
    wspec = lambda shape, which: pl.BlockSpec(
        (None,) + shape, lambda j, ti, el, eh, ac: ((el, eh)[which][j], 0, 0))
    grid_spec = pltpu.PrefetchScalarGridSpec(
        num_scalar_prefetch=4,
        grid=(MOE_ITEMS,),
        in_specs=[pl.BlockSpec((rows, LANES), lambda j, ti, el, eh, ac: (ti[j], 0)),
                  wspec((D_MODEL, D_EXPERT), 0), wspec((D_MODEL, D_EXPERT), 0), wspec((D_EXPERT, D_MODEL), 0),
                  wspec((D_MODEL, D_EXPERT), 1), wspec((D_MODEL, D_EXPERT), 1), wspec((D_EXPERT, D_MODEL), 1)],
        out_specs=pl.BlockSpec((rows, LANES), lambda j, ti, el, eh, ac: (j, 0)),
        scratch_shapes=[pltpu.VMEM((4, D_MODEL, D_EXPERT), BF16),
                        pltpu.VMEM((2, D_EXPERT, D_MODEL), BF16)],
    )
    return pl.pallas_call(
        _moe_kernel,
        grid_spec=grid_spec,
        out_shape=jax.ShapeDtypeStruct((MOE_ITEMS * rows, LANES), F32),
        compiler_params=_cparams("arbitrary"),
        name="moe",
    )(in_tile, e_lo, e_hi, flags, xs, w1, w3, w2, w1, w3, w2)


def _ple_ln_kernel(pos_ref, pos_next_ref, x_ref, ys_hbm, p_ref, wpg_ref, bpg_ref, wp_ref, g_ref, b_ref,
                   o_ref, fbuf, sem):
    i = pl.program_id(0)
    n = pl.num_programs(0)
    tm = DISPATCH_TM
    slot = i % 2

    def row_gather(idx_ref, s):
        def body(h, carry):
            for queue in range(DMA_QUEUES):
                r = h * DMA_QUEUES + queue
                src = pl.multiple_of(idx_ref[0, r] * TOK_ROWS, TOK_ROWS)
                dst = pl.multiple_of(r * TOK_ROWS, TOK_ROWS)
                pltpu.make_async_copy(ys_hbm.at[pl.ds(src, TOK_ROWS)], fbuf.at[s, pl.ds(dst, TOK_ROWS)],
                                      sem.at[s]).start(priority=queue)
            return carry
        lax.fori_loop(0, tm // DMA_QUEUES, body, 0, unroll=4)

    def slab_wait(s):
        pltpu.make_async_copy(ys_hbm.at[pl.ds(0, tm * TOK_ROWS)], fbuf.at[s], sem.at[s]).wait()

    @pl.when(i == 0)
    def _():
        row_gather(pos_ref, slot)

    x = x_ref[...]
    xb = x.astype(BF16)
    n_blocks = D_MODEL // PLE_BN
    per_block = tm // n_blocks
    gate_blocks = []
    for c in range(n_blocks):
        for r in range(per_block * c, per_block * (c + 1)):
            src = pl.multiple_of(pos_next_ref[0, r] * TOK_ROWS, TOK_ROWS)
            pltpu.make_async_copy(ys_hbm.at[pl.ds(src, TOK_ROWS)],
                                  fbuf.at[1 - slot, pl.ds(r * TOK_ROWS, TOK_ROWS)],
                                  sem.at[1 - slot]).start(priority=r % DMA_QUEUES)
        cs = slice(PLE_BN * c, PLE_BN * (c + 1))
        gate_blocks.append(jax.nn.sigmoid(
            jnp.dot(xb, wpg_ref[:, cs], preferred_element_type=F32) + bpg_ref[:, cs]))
    gate = jnp.concatenate(gate_blocks, axis=1)
    pe = jnp.dot(p_ref[...].astype(BF16), wp_ref[...], preferred_element_type=F32)
    base = DN_ALPHA * x + gate * pe

    slab_wait(slot)

    @pl.when(i == n - 1)
    def _():
        slab_wait(1 - slot)

    ffn = jnp.concatenate([fbuf[slot, pl.ds(k, tm, stride=TOK_ROWS), :] for k in range(N_LANE_CHUNKS)], axis=1)
    o_ref[...] = _layer_norm(base + ffn, g_ref[...], b_ref[...])


def _ple_ln_call(pos, x, ys, p, wpg, bpg, wp, g, b):
    tm = DISPATCH_TM
    nt = TOKENS // tm
    row = lambda n: pl.BlockSpec((tm, n), lambda i: (i, 0))
    const = lambda *shape: pl.BlockSpec(shape, lambda i: (0,) * len(shape))
    return pl.pallas_call(
        _ple_ln_kernel,
        grid=(nt,),
        in_specs=[pl.BlockSpec((None, 1, tm), lambda i: (i, 0, 0), memory_space=pltpu.SMEM),
                  pl.BlockSpec((None, 1, tm), lambda i: (jnp.minimum(i + 1, nt - 1), 0, 0),
                               memory_space=pltpu.SMEM),
                  row(D_MODEL), pl.BlockSpec(memory_space=pl.ANY), row(D_PLE),
                  const(D_MODEL, D_MODEL), const(1, D_MODEL), const(D_PLE, D_MODEL),
                  const(1, D_MODEL), const(1, D_MODEL)],
        out_specs=row(D_MODEL),
        out_shape=jax.ShapeDtypeStruct((TOKENS, D_MODEL), F32),
        scratch_shapes=[pltpu.VMEM((2, tm * TOK_ROWS, LANES), F32),
                        pltpu.SemaphoreType.DMA((2,))],
        compiler_params=_cparams("arbitrary"),
        name="ple_ln",
    )(pos, pos, x, ys, p, wpg, bpg, wp, g, b)


def _dup_heads(w):
    lead = w.shape[:-1]
    w4 = w.reshape(lead + (N_KV_HEADS, 1, HEAD_DIM))
    return jnp.broadcast_to(w4, lead + (N_KV_HEADS, 2, HEAD_DIM)).reshape(lead + (2 * KV_CH,))


def _row(v):
    return v.reshape(1, -1)


def kernel(x, p, ln_emb_g, ln_emb_b, w_in, b_in, conv_w, conv_b, conv_ln_g, conv_ln_b, w_conv_out, w_attn_out, attn_sink, w_out, ln1_g, ln1_b, w_router_group, b_router_group, w_router_expert, b_router_expert, w1, w3, w2, w_p, w_pg, b_pg, ln2_g, ln2_b):
    c0, c1, c2, c3, c4 = (2 * CONV_CH, 2 * CONV_CH + D_MODEL, 2 * CONV_CH + D_MODEL + KV_CH,
                          2 * CONV_CH + D_MODEL + 2 * KV_CH, 2 * CONV_CH + 2 * D_MODEL + 2 * KV_CH)
    q_scale = HEAD_DIM ** -0.5 * LOG2E
    router_tm = 512

    w1_all = w1.reshape(DEPTH * N_EXPERTS, D_MODEL, D_EXPERT)
    w3_all = w3.reshape(DEPTH * N_EXPERTS, D_MODEL, D_EXPERT)
    w2_all = w2.reshape(DEPTH * N_EXPERTS, D_EXPERT, D_MODEL)
    xs = _ln_call(x.reshape(TOKENS, D_MODEL), _row(ln_emb_g), _row(ln_emb_b))
    for i in range(DEPTH):
        w, b = w_in[i], b_in[i]
        wq = (w[:, c0:c1] * q_scale).astype(BF16)
        bq = _row(b[c0:c1] * q_scale)
        wkv = jnp.concatenate([_dup_heads(w[:, c1:c2]), _dup_heads(w[:, c2:c3])], axis=1)
        bkv = jnp.concatenate([_dup_heads(b[c1:c2]), _dup_heads(b[c2:c3])])
        wgate = (w[:, c3:] * -LOG2E).astype(BF16)
        bgate = _row(b[c3:] * -LOG2E)
        blocks = lambda m: jnp.stack([m[..., :CONV_CH].reshape(m.shape[:-1] + (-1, GLU_BN)),
                                      m[..., CONV_CH:c0].reshape(m.shape[:-1] + (-1, GLU_BN)),
                                      m[..., c0:].reshape(m.shape[:-1] + (-1, GLU_BN))], axis=-2)
        w_glu = blocks(jnp.concatenate([w[:, :CONV_CH], w[:, CONV_CH:c0] * -LOG2E, wkv], axis=1))
        w_glu = w_glu.reshape(D_MODEL, -1).astype(BF16)
        b_glu = blocks(jnp.concatenate([b[:CONV_CH], b[CONV_CH:c0] * -LOG2E, bkv])).reshape(1, -1)
        cw16 = lax.bitcast_convert_type(jnp.pad(conv_w[i], ((0, 1), (0, 0))).astype(BF16), jnp.uint16)
        cw = cw16.astype(jnp.uint32) * jnp.uint32(0x00010001)
        cw = cw.reshape(32, N_LANE_CHUNKS, LANES).transpose(1, 0, 2)

        yc, kv = _glu_conv_call(xs, w_glu, b_glu, cw, _row(conv_b[i]))
        x1 = _mixer_call(attn_sink[i].astype(F32) * LOG2E, xs, yc, kv.reshape(BATCH, SEQ, 4 * KV_CH),
                         wq, bq, wgate, bgate, _row(conv_ln_g[i]), _row(conv_ln_b[i]),
                         w_conv_out[i].astype(BF16), w_attn_out[i].astype(BF16), w_out[i].astype(BF16),
                         _row(ln1_g[i]), _row(ln1_b[i]))

        wr = jnp.concatenate([w_router_group[i], w_router_expert[i]], axis=1).T
        wr = jnp.pad(wr, ((0, ROUTER_ROWS - wr.shape[0]), (0, 0)))
        wr_hi = wr.astype(BF16)
        wr_lo = (wr - wr_hi.astype(F32)).astype(BF16)
        br = jnp.pad(jnp.concatenate([b_router_group[i], b_router_expert[i]]), (0, ROUTER_ROWS - 20))
        br = jnp.broadcast_to(br[:, None], (ROUTER_ROWS, router_tm))
        meta, cnt = _router_call(x1, jnp.stack([wr_hi, wr_lo]), br, tm=router_tm)

        tables = _dispatch_tables(meta, cnt)
        sorted_x = _dispatch_call(tables, x1, meta)
        sorted_y = _moe_call(tables, sorted_x, w1_all, w3_all, w2_all, i)
        xs = _ple_ln_call(tables[0], x1, sorted_y, p[i].reshape(TOKENS, D_PLE), w_pg[i].astype(BF16),
                          _row(b_pg[i]), w_p[i].astype(BF16), _row(ln2_g[i]), _row(ln2_b[i]))
    return xs.reshape(BATCH, SEQ, D_MODEL)
```

```python
import functools

import jax
import jax.numpy as jnp
from jax import lax
from jax.experimental import pallas as pl
from jax.experimental.pallas import tpu as pltpu

D_MODEL = 1024
BATCH = 16
SEQ = 2048
DEPTH = 2
TOKENS = BATCH * SEQ
CONV_CH = D_MODEL
CONV_WIDTH = 31
CONV_PAD = CONV_WIDTH // 2
HEAD_DIM = 64
N_Q_HEADS = 16
N_KV_HEADS = 4
GQA_GROUP = 4
KV_CH = N_KV_HEADS * HEAD_DIM
WINDOW = 128
BLOCK = 128
SPAN = BLOCK + 2 * WINDOW
N_GROUPS = 4
EXPERTS_PER_GROUP = 4
N_EXPERTS = N_GROUPS * EXPERTS_PER_GROUP
D_EXPERT = 512
D_PLE = 256
DN_ALPHA = (2 * DEPTH) ** 0.25
LN_EPS = 1e-5
NEG_INF = -1e30
LOG2E = 1.4426950408889634

LANES = 128
N_LANE_CHUNKS = D_MODEL // LANES
VMEM_LIMIT = 56 * 1024 * 1024

F32 = jnp.float32
BF16 = jnp.bfloat16


def _layer_norm(v, g, b):
    mu = jnp.mean(v, axis=-1, keepdims=True)
    d = v - mu
    var = jnp.mean(d * d, axis=-1, keepdims=True)
    return d * lax.rsqrt(var + LN_EPS) * g + b


def _sigmoid_from_neg_log2(z):
    return 1.0 / (1.0 + jnp.exp2(z))


def _cparams(*sem):
    return pltpu.CompilerParams(dimension_semantics=sem, vmem_limit_bytes=VMEM_LIMIT)


def _ln_kernel(x_ref, g_ref, b_ref, o_ref):
    o_ref[...] = _layer_norm(x_ref[...], g_ref[...], b_ref[...])


def _ln_call(x, g, b, tm=512):
    return pl.pallas_call(
        _ln_kernel,
        grid=(TOKENS // tm,),
        in_specs=[pl.BlockSpec((tm, D_MODEL), lambda i: (i, 0)),
                  pl.BlockSpec((1, D_MODEL), lambda i: (0, 0)),
                  pl.BlockSpec((1, D_MODEL), lambda i: (0, 0))],
        out_specs=pl.BlockSpec((tm, D_MODEL), lambda i: (i, 0)),
        out_shape=jax.ShapeDtypeStruct((TOKENS, D_MODEL), F32),
        compiler_params=_cparams("arbitrary"),
        name="ln_emb",
    )(x, g, b)


GLU_BN = 256
GLU_PIECE = 512
CONV_RBW = 32
HALO = 16
PAIRS = (SEQ + 2 * HALO) // 2
HI_MASK = 0xFFFF0000


def _bits(v):
    return lax.bitcast_convert_type(v, jnp.uint32)


def _pack_pair(lo_bits, hi_bits):
    return lax.shift_right_logical(lo_bits, jnp.uint32(16)) | (hi_bits & jnp.uint32(HI_MASK))


def _glu_conv_kernel(x_ref, w_ref, b_ref, cw_ref, cb_ref, y_ref, kv_ref, ubuf, pbuf, ybuf):
    n_chunks = GLU_BN // LANES
    n_pieces = SEQ // GLU_PIECE
    max_shift = (CONV_WIDTH - 1 + HALO - CONV_PAD) // 2
    for k in range(n_chunks):
        ubuf[k, 0:HALO, :] = jnp.zeros((HALO, LANES), F32)
        ubuf[k, HALO + SEQ:2 * HALO + SEQ, :] = jnp.zeros((HALO, LANES), F32)

    def project(i):
        rows = slice(GLU_PIECE * i, GLU_PIECE * (i + 1))
        acc = jnp.dot(x_ref[rows, :].astype(BF16), w_ref[...], preferred_element_type=F32) + b_ref[...]
        a, g = acc[:, :GLU_BN], acc[:, GLU_BN:2 * GLU_BN]
        kv_ref[rows, :] = acc[:, 2 * GLU_BN:].astype(BF16)
        u = (a * _sigmoid_from_neg_log2(g)).astype(BF16).astype(F32)
        for k in range(n_chunks):
            ubuf[k, HALO + GLU_PIECE * i:HALO + GLU_PIECE * (i + 1), :] = u[:, LANES * k:LANES * (k + 1)]

    def packed_upto(i):
        return PAIRS if i == n_pieces - 1 else GLU_PIECE // 2 * (i + 1)

    def pack(i):
        w0 = 0 if i == 0 else packed_upto(i - 1)
        n = packed_upto(i) - w0
        n_odd = n - 1 if i == n_pieces - 1 else n
        for k in range(n_chunks):
            even = _bits(ubuf[k, pl.ds(2 * w0, n, stride=2), :])
            odd = _bits(ubuf[k, pl.ds(2 * w0 + 1, n, stride=2), :])
            even_next = _bits(ubuf[k, pl.ds(2 * w0 + 2, n_odd, stride=2), :])
            pbuf[0, k, w0:w0 + n, :] = _pack_pair(even, odd)
            pbuf[1, k, w0:w0 + n_odd, :] = _pack_pair(odd[0:n_odd], even_next)

    def conv_block(rb):
        m0 = rb * CONV_RBW
        accs = [[jnp.zeros((2 * CONV_RBW, LANES), BF16) for _ in range(2)] for _ in range(n_chunks)]
        for j in range(CONV_WIDTH):
            d = j + HALO - CONV_PAD
            for k in range(n_chunks):
                word = pbuf[d % 2, k, m0 + d // 2:m0 + d // 2 + CONV_RBW, :]
                tap = pltpu.bitcast(jnp.broadcast_to(cw_ref[k, j:j + 1, :], (CONV_RBW, LANES)), BF16)
                accs[k][d % 2] = accs[k][d % 2] + pltpu.bitcast(word, BF16) * tap
        for k in range(n_chunks):
            cs = slice(LANES * k, LANES * (k + 1))
            aw = pltpu.bitcast(accs[k][0] + accs[k][1], jnp.uint32)
            bias = cb_ref[:, cs]
            ybuf[k, pl.ds(2 * m0, CONV_RBW, stride=2), :] = (
                lax.bitcast_convert_type(lax.shift_left(aw, jnp.uint32(16)), F32) + bias)
            ybuf[k, pl.ds(2 * m0 + 1, CONV_RBW, stride=2), :] = (
                lax.bitcast_convert_type(aw & jnp.uint32(HI_MASK), F32) + bias)
            y_ref[2 * m0:2 * (m0 + CONV_RBW), cs] = ybuf[k, 2 * m0:2 * (m0 + CONV_RBW), :].astype(BF16)

    n_blocks = SEQ // 2 // CONV_RBW
    ready = [min(n_blocks, (packed_upto(i) - max_shift - CONV_RBW) // CONV_RBW + 1) for i in range(n_pieces)]
    project(0)
    pack(0)
    done = 0
    for i in range(1, n_pieces):
        project(i)
        for rb in range(done, ready[i - 1]):
            conv_block(rb)
        done = ready[i - 1]
        pack(i)
    for rb in range(done, n_blocks):
        conv_block(rb)


def _glu_conv_call(x, w, b, cw, cb):
    bn = GLU_BN
    n_chunks = bn // LANES
    return pl.pallas_call(
        _glu_conv_kernel,
        grid=(BATCH, CONV_CH // bn),
        in_specs=[pl.BlockSpec((SEQ, D_MODEL), lambda b, j: (b, 0)),
                  pl.BlockSpec((D_MODEL, 3 * bn), lambda b, j: (0, j)),
                  pl.BlockSpec((1, 3 * bn), lambda b, j: (0, j)),
                  pl.BlockSpec((n_chunks, 32, LANES), lambda b, j: (j, 0, 0)),
                  pl.BlockSpec((1, bn), lambda b, j: (0, j))],
        out_specs=[pl.BlockSpec((SEQ, bn), lambda b, j: (b, j)),
                   pl.BlockSpec((SEQ, bn), lambda b, j: (b, j))],
        out_shape=[jax.ShapeDtypeStruct((TOKENS, CONV_CH), BF16),
                   jax.ShapeDtypeStruct((TOKENS, 4 * KV_CH), BF16)],
        scratch_shapes=[pltpu.VMEM((n_chunks, SEQ + 2 * HALO, LANES), F32),
                        pltpu.VMEM((2, n_chunks, PAIRS, LANES), jnp.uint32),
                        pltpu.VMEM((n_chunks, SEQ, LANES), F32)],
        compiler_params=_cparams("arbitrary", "arbitrary"),
        name="glu_conv",
    )(x, w, b, cw, cb)


MIX_TS = 512
N_VARIANTS = 3


def _alibi_slope(h):
    return 2.0 ** (-8.0 * (h + 1) / N_Q_HEADS)


def _mixer_kernel(sink_ref,
                  x_ref, yc_ref, kv_ref, wq_ref, bq_ref, wg_ref, bg_ref,
                  clg_ref, clb_ref,
                  wco_ref, wao_ref, wo_ref, l1g_ref, l1b_ref,
                  o_ref,
                  obuf, bias_tab):
    b = pl.program_id(0)
    s = pl.program_id(1)
    ts = MIX_TS
    t0 = pl.multiple_of(s * ts, ts)

    @pl.when((b == 0) & (s == 0))
    def _():
        r = lax.broadcasted_iota(jnp.int32, (BLOCK, SPAN), 0)
        kk = lax.broadcasted_iota(jnp.int32, (BLOCK, SPAN), 1)
        for v in range(N_VARIANTS):
            dist = jnp.abs(kk - r - v * WINDOW)
            inside = dist <= WINDOW
            distf = dist.astype(F32)
            for h in range(N_Q_HEADS):
                bias_tab[v * N_Q_HEADS + h] = jnp.where(inside, -(_alibi_slope(h) * LOG2E) * distf, NEG_INF)

    xt = x_ref[...]
    xb = xt.astype(BF16)
    q = (jnp.dot(xb, wq_ref[...], preferred_element_type=F32) + bq_ref[...]).astype(BF16)

    yn = _layer_norm(yc_ref[...].astype(F32), clg_ref[...], clb_ref[...])
    act = (yn * jax.nn.sigmoid(yn)).astype(BF16)
    y_conv = jnp.dot(act, wco_ref[...], preferred_element_type=F32)

    low_half = lax.broadcasted_iota(jnp.int32, (BLOCK, LANES), 1) < HEAD_DIM
    zero_q = jnp.zeros((BLOCK, LANES), BF16)
    nt_dims = (((1,), (1,)), ((), ()))
    n_qblocks = ts // BLOCK

    def window_start(qi):
        return pl.multiple_of(jnp.clip(t0 + BLOCK * qi - WINDOW, 0, SEQ - SPAN), BLOCK)

    def scores(qi):
        rows = slice(BLOCK * qi, BLOCK * (qi + 1))
        ws = window_start(qi)
        out = []
        for h in range(N_KV_HEADS):
            kw = kv_ref[pl.ds(ws, SPAN), LANES * h:LANES * (h + 1)]
            qc = [q[rows, 2 * LANES * h:2 * LANES * h + LANES],
                  q[rows, 2 * LANES * h + LANES:2 * LANES * (h + 1)]]
            qst = jnp.concatenate([jnp.where(low_half, qc[0], zero_q), jnp.where(low_half, zero_q, qc[0]),
                                   jnp.where(low_half, qc[1], zero_q), jnp.where(low_half, zero_q, qc[1])], axis=0)
            out.append(lax.dot_general(qst, kw, nt_dims, preferred_element_type=F32))
        return out

    def softmax(qi, sc):
        qs = t0 + BLOCK * qi
        variant = jnp.where(qs == 0, 0, jnp.where(qs == SEQ - BLOCK, 2, 1))
        probs, dens = [], []
        for h in range(N_KV_HEADS):
            ph = []
            for g in range(GQA_GROUP):
                hq = GQA_GROUP * h + g
                sg = sc[h][BLOCK * g:BLOCK * (g + 1), :] + bias_tab[variant * N_Q_HEADS + hq]
                sink = sink_ref[hq]
                m = jnp.maximum(jnp.max(sg, axis=-1, keepdims=True), sink)
                p = jnp.exp2(sg - m)
                dens.append(jnp.sum(p, axis=-1, keepdims=True) + jnp.exp2(sink - m))
                ph.append(p.astype(BF16))
            probs.append(jnp.concatenate(ph, axis=0))
        return probs, dens

    def weighted_values(qi, probs, dens):
        rows = slice(BLOCK * qi, BLOCK * (qi + 1))
        ws = window_start(qi)
        for h in range(N_KV_HEADS):
            vw = kv_ref[pl.ds(ws, SPAN), 2 * KV_CH + LANES * h:2 * KV_CH + LANES * (h + 1)]
            pv = jnp.dot(probs[h], vw, preferred_element_type=F32)
            og = [pv[BLOCK * g:BLOCK * (g + 1), :] / dens[GQA_GROUP * h + g] for g in range(GQA_GROUP)]
            obuf[rows, 2 * LANES * h:2 * LANES * h + LANES] = jnp.where(low_half, og[0], og[1]).astype(BF16)
            obuf[rows, 2 * LANES * h + LANES:2 * LANES * (h + 1)] = jnp.where(low_half, og[2], og[3]).astype(BF16)

    n_gate_blocks = 2 * D_MODEL // n_qblocks
    gate_blocks = []
    sc_next = scores(0)
    for qi in range(n_qblocks):
        sc_cur = sc_next
        if qi + 1 < n_qblocks:
            sc_next = scores(qi + 1)
        gs = slice(n_gate_blocks * qi, n_gate_blocks * (qi + 1))
        gate_blocks.append(_sigmoid_from_neg_log2(
            jnp.dot(xb, wg_ref[:, gs], preferred_element_type=F32) + bg_ref[:, gs]))
        probs, dens = softmax(qi, sc_cur)
        weighted_values(qi, probs, dens)
    y_attn = jnp.dot(obuf[...], wao_ref[...], preferred_element_type=F32)

    gates = jnp.concatenate(gate_blocks, axis=1)
    merged = gates[:, :D_MODEL] * y_conv + gates[:, D_MODEL:] * y_attn
    out = jnp.dot(merged.astype(BF16), wo_ref[...], preferred_element_type=F32)
    o_ref[...] = _layer_norm(DN_ALPHA * xt + out, l1g_ref[...], l1b_ref[...])


def _mixer_call(sink, x, yc, kv, wq, bq, wg, bg, clg, clb, wco, wao, wo, l1g, l1b):
    ts = MIX_TS
    ns = SEQ // ts
    const = lambda *shape: pl.BlockSpec(shape, lambda b, s, sk: (0,) * len(shape),
                                        pipeline_mode=pl.Buffered(1))
    grid_spec = pltpu.PrefetchScalarGridSpec(
        num_scalar_prefetch=1,
        grid=(BATCH, ns),
        in_specs=[pl.BlockSpec((ts, D_MODEL), lambda b, s, sk: (b * ns + s, 0)),
                  pl.BlockSpec((ts, CONV_CH), lambda b, s, sk: (b * ns + s, 0)),
                  pl.BlockSpec((None, SEQ, 4 * KV_CH), lambda b, s, sk: (b, 0, 0),
                               pipeline_mode=pl.Buffered(1)),
                  const(D_MODEL, D_MODEL), const(1, D_MODEL),
                  const(D_MODEL, 2 * D_MODEL), const(1, 2 * D_MODEL),
                  const(1, CONV_CH), const(1, CONV_CH),
                  const(CONV_CH, D_MODEL), const(D_MODEL, D_MODEL), const(D_MODEL, D_MODEL),
                  const(1, D_MODEL), const(1, D_MODEL)],
        out_specs=pl.BlockSpec((ts, D_MODEL), lambda b, s, sk: (b * ns + s, 0)),
        scratch_shapes=[pltpu.VMEM((ts, D_MODEL), BF16),
                        pltpu.VMEM((N_VARIANTS * N_Q_HEADS, BLOCK, SPAN), F32)],
    )
    return pl.pallas_call(
        _mixer_kernel,
        grid_spec=grid_spec,
        out_shape=jax.ShapeDtypeStruct((TOKENS, D_MODEL), F32),
        compiler_params=_cparams("arbitrary", "arbitrary"),
        name="mixer",
    )(sink, x, yc, kv, wq, bq, wg, bg, clg, clb, wco, wao, wo, l1g, l1b)


ROUTER_ROWS = 32


N_PAIRS = 6
N_CLASSES = N_GROUPS * N_PAIRS
PAIR_LO = (0, 0, 0, 1, 1, 2)
PAIR_HI = (1, 2, 3, 2, 3, 3)
META_ROWS = 8


def _router_kernel(x_ref, w_ref, b_ref, meta_ref, cnt_ref, carry_ref, upper_ref):
    @pl.when(pl.program_id(0) == 0)
    def _():
        carry_ref[...] = jnp.zeros_like(carry_ref)
        src = lax.broadcasted_iota(jnp.int32, upper_ref.shape, 0)
        dst = lax.broadcasted_iota(jnp.int32, upper_ref.shape, 1)
        upper_ref[...] = jnp.where(src <= dst, 1.0, 0.0).astype(BF16)

    x = x_ref[...]
    tm = x.shape[0]
    xh = x.astype(BF16)
    xl = (x - xh.astype(F32)).astype(BF16)
    nt_dims = (((1,), (1,)), ((), ()))
    lt = (lax.dot_general(w_ref[0], xh, nt_dims, preferred_element_type=F32)
          + lax.dot_general(w_ref[1], xh, nt_dims, preferred_element_type=F32)
          + lax.dot_general(w_ref[0], xl, nt_dims, preferred_element_type=F32))
    lt = lt + b_ref[...]
    row4 = lax.broadcasted_iota(jnp.int32, (N_GROUPS, tm), 0)
    gl = lt[0:N_GROUPS, :]
    gmax = jnp.max(gl, axis=0, keepdims=True)
    g_w = 1.0 / jnp.sum(jnp.exp(gl - gmax), axis=0, keepdims=True)
    g_idx = jnp.min(jnp.where(gl == gmax, row4, N_GROUPS), axis=0, keepdims=True)
    e_sel = jnp.zeros((EXPERTS_PER_GROUP, tm), F32)
    for g in range(N_GROUPS):
        lo = N_GROUPS + EXPERTS_PER_GROUP * g
        e_sel = e_sel + jnp.where(g_idx == g, lt[lo:lo + EXPERTS_PER_GROUP, :], 0.0)
    e1 = jnp.max(e_sel, axis=0, keepdims=True)
    i1 = jnp.min(jnp.where(e_sel == e1, row4, EXPERTS_PER_GROUP), axis=0, keepdims=True)
    rest = jnp.where(row4 == i1, -jnp.inf, e_sel)
    e2 = jnp.max(rest, axis=0, keepdims=True)
    i2 = jnp.min(jnp.where(rest == e2, row4, EXPERTS_PER_GROUP), axis=0, keepdims=True)
    t = jnp.exp(e2 - e1)
    w1 = (1.0 / (1.0 + t)) * g_w
    w2 = (t / (1.0 + t)) * g_w
    first_lower = i1 < i2
    e_lo = jnp.minimum(i1, i2)
    e_hi = jnp.maximum(i1, i2)
    w_lo = jnp.where(first_lower, w1, w2)
    w_hi = jnp.where(first_lower, w2, w1)
    pair = jnp.where(e_lo == 0, 0, jnp.where(e_lo == 1, 3, 5)) + e_hi - e_lo - 1
    cls = g_idx * N_PAIRS + pair

    row32 = lax.broadcasted_iota(jnp.int32, (ROUTER_ROWS, tm), 0)
    member = row32 == cls
    onehot = jnp.where(member, 1.0, 0.0)
    prefix = jnp.dot(onehot.astype(BF16), upper_ref[...], preferred_element_type=F32)
    carry = carry_ref[:, 0:1]
    rank = jnp.sum(jnp.where(member, prefix - 1.0 + carry, 0.0), axis=0, keepdims=True)
    carry_ref[...] = carry_ref[...] + jnp.sum(onehot, axis=1, keepdims=True)
    cnt_ref[...] = carry_ref[...]

    meta_ref[...] = jnp.concatenate(
        [cls.astype(F32), rank, w_lo, w_hi, jnp.zeros((META_ROWS - 4, tm), F32)], axis=0)


def _router_call(x, w, b, tm=512):
    return pl.pallas_call(
        _router_kernel,
        grid=(TOKENS // tm,),
        in_specs=[pl.BlockSpec((tm, D_MODEL), lambda i: (i, 0)),
                  pl.BlockSpec((2, ROUTER_ROWS, D_MODEL), lambda i: (0, 0, 0)),
                  pl.BlockSpec((ROUTER_ROWS, tm), lambda i: (0, 0))],
        out_specs=[pl.BlockSpec((META_ROWS, tm), lambda i: (0, i)),
                   pl.BlockSpec((ROUTER_ROWS, LANES), lambda i: (0, 0))],
        out_shape=[jax.ShapeDtypeStruct((META_ROWS, TOKENS), F32),
                   jax.ShapeDtypeStruct((ROUTER_ROWS, LANES), F32)],
        scratch_shapes=[pltpu.VMEM((ROUTER_ROWS, LANES), F32),
                        pltpu.VMEM((tm, tm), BF16)],
        compiler_params=_cparams("arbitrary"),
        name="router",
    )(x, w, b)


MOE_TM = 256
TOK_ROWS = 8
MOE_ITEMS = TOKENS // MOE_TM + N_CLASSES
SORTED_TILES = MOE_ITEMS + 1
SORTED_TOKENS = SORTED_TILES * MOE_TM
MAX_IDLE_TILES = SORTED_TILES - TOKENS // MOE_TM
DISPATCH_TM = 512
FLAG_ACTIVE, FLAG_NEW_EXPERTS = 1, 2
DMA_QUEUES = 2
PLE_BN = 256
DISPATCH_CHUNKS = 4


def _dispatch_tables(meta, cnt):
    cls = meta[0].astype(jnp.int32)
    rank = meta[1].astype(jnp.int32)
    counts = cnt[:N_CLASSES, 0].astype(jnp.int32)
    tiles_per_class = (counts + MOE_TM - 1) // MOE_TM
    tile_end = jnp.cumsum(tiles_per_class)
    starts = (tile_end - tiles_per_class) * MOE_TM
    pos = starts[cls] + rank
    fill_from = starts + counts
    n_tiles = tile_end[-1]
    j = jnp.arange(MOE_ITEMS, dtype=jnp.int32)
    active = j < n_tiles
    jj = jnp.minimum(j, n_tiles - 1)
    c = jnp.minimum(jnp.sum(tile_end[None, :] <= jj[:, None], axis=1), N_CLASSES - 1)
    grp, pair = c // N_PAIRS, c % N_PAIRS
    e_lo = grp * EXPERTS_PER_GROUP + jnp.asarray(PAIR_LO, jnp.int32)[pair]
    e_hi = grp * EXPERTS_PER_GROUP + jnp.asarray(PAIR_HI, jnp.int32)[pair]
    i32 = lambda v: v.astype(jnp.int32)
    fill = jnp.concatenate([i32(fill_from), i32(n_tiles).reshape(1)])
    new_experts = jnp.concatenate([jnp.ones((1,), bool), c[1:] != c[:-1]])
    flags = i32(active) * FLAG_ACTIVE + i32(new_experts) * FLAG_NEW_EXPERTS
    return (i32(pos).reshape(TOKENS // DISPATCH_TM, 1, DISPATCH_TM), fill,
            i32(jj), i32(e_lo), i32(e_hi), flags)


def _dispatch_kernel(fill_ref, pos_ref, x_ref, meta_ref, xs_hbm, stage, zeros, sem, zsem):
    i = pl.program_id(0)
    n = pl.num_programs(0)
    tm = DISPATCH_TM
    slot = i % 2

    def slab_wait(s):
        pltpu.make_async_copy(stage.at[s], xs_hbm.at[pl.ds(0, tm * TOK_ROWS)], sem.at[s]).wait()

    @pl.when(i == 0)
    def _():
        zeros[...] = jnp.zeros_like(zeros)
        for c in range(N_CLASSES):
            dst = pl.multiple_of(fill_ref[c] * TOK_ROWS, TOK_ROWS)
            pltpu.make_async_copy(zeros, xs_hbm.at[pl.ds(dst, MOE_TM * TOK_ROWS)], zsem.at[0]).start()
        for c in range(N_CLASSES):
            pltpu.make_async_copy(zeros, xs_hbm.at[pl.ds(0, MOE_TM * TOK_ROWS)], zsem.at[0]).wait()
        for k in range(MAX_IDLE_TILES):
            t = fill_ref[N_CLASSES] + k

            @pl.when(t < SORTED_TILES)
            def _():
                dst = pl.multiple_of(t * (MOE_TM * TOK_ROWS), MOE_TM * TOK_ROWS)
                tail = pltpu.make_async_copy(zeros, xs_hbm.at[pl.ds(dst, MOE_TM * TOK_ROWS)], zsem.at[0])
                tail.start()
                tail.wait()

    @pl.when(i >= 2)
    def _():
        slab_wait(slot)

    wrows = jnp.concatenate([meta_ref[2:4, :], jnp.zeros((LANES - 2, tm), F32)], axis=0)
    wcols = lax.bitcast_convert_type(wrows.T, jnp.uint32)
    chunk = tm // DISPATCH_CHUNKS
    for c in range(DISPATCH_CHUNKS):
        rows = slice(chunk * c, chunk * (c + 1))
        first = chunk * c * TOK_ROWS
        for g in range(D_MODEL // (2 * LANES)):
            lo = x_ref[rows, 2 * LANES * g:2 * LANES * g + LANES].astype(BF16).astype(F32)
            hi = x_ref[rows, 2 * LANES * g + LANES:2 * LANES * (g + 1)].astype(BF16).astype(F32)
            stage[slot, pl.ds(first + g, chunk, stride=TOK_ROWS), :] = _pack_pair(_bits(lo), _bits(hi))
        stage[slot, pl.ds(first + 4, chunk, stride=TOK_ROWS), :] = wcols[rows, :]
        for g in range(5, TOK_ROWS):
            stage[slot, pl.ds(first + g, chunk, stride=TOK_ROWS), :] = jnp.zeros((chunk, LANES), jnp.uint32)
        for r in range(chunk * c, chunk * (c + 1)):
            dst = pl.multiple_of(pos_ref[0, r] * TOK_ROWS, TOK_ROWS)
            pltpu.make_async_copy(stage.at[slot, pl.ds(r * TOK_ROWS, TOK_ROWS)],
                                  xs_hbm.at[pl.ds(dst, TOK_ROWS)],
                                  sem.at[slot]).start(priority=r % DMA_QUEUES)

    @pl.when(i == n - 1)
    def _():
        slab_wait(1 - slot)
        slab_wait(slot)


def _router_pack_kernel(x_ref, w_ref, b_ref, meta_ref, cnt_ref, xp_ref, carry_ref, upper_ref):
    _router_kernel(x_ref, w_ref, b_ref, meta_ref, cnt_ref, carry_ref, upper_ref)
    tm = x_ref.shape[0]
    wrows = jnp.concatenate([meta_ref[2:4, :], jnp.zeros((LANES - 2, tm), F32)], axis=0)
    for g in range(D_MODEL // (2 * LANES)):
        lo = x_ref[:, 2 * LANES * g:2 * LANES * g + LANES].astype(BF16).astype(F32)
        hi = x_ref[:, 2 * LANES * g + LANES:2 * LANES * (g + 1)].astype(BF16).astype(F32)
        xp_ref[pl.ds(g, tm, stride=TOK_ROWS), :] = _pack_pair(_bits(lo), _bits(hi))
    xp_ref[pl.ds(4, tm, stride=TOK_ROWS), :] = lax.bitcast_convert_type(wrows.T, jnp.uint32)
    for g in range(5, TOK_ROWS):
        xp_ref[pl.ds(g, tm, stride=TOK_ROWS), :] = jnp.zeros((tm, LANES), jnp.uint32)


def _router_pack_call(x, w, b, tm=512):
    return pl.pallas_call(
        _router_pack_kernel,
        grid=(TOKENS // tm,),
        in_specs=[pl.BlockSpec((tm, D_MODEL), lambda i: (i, 0)),
                  pl.BlockSpec((2, ROUTER_ROWS, D_MODEL), lambda i: (0, 0, 0)),
                  pl.BlockSpec((ROUTER_ROWS, tm), lambda i: (0, 0))],
        out_specs=[pl.BlockSpec((META_ROWS, tm), lambda i: (0, i)),
                   pl.BlockSpec((ROUTER_ROWS, LANES), lambda i: (0, 0)),
                   pl.BlockSpec((tm * TOK_ROWS, LANES), lambda i: (i, 0))],
        out_shape=[jax.ShapeDtypeStruct((META_ROWS, TOKENS), F32),
                   jax.ShapeDtypeStruct((ROUTER_ROWS, LANES), F32),
                   jax.ShapeDtypeStruct((TOKENS * TOK_ROWS, LANES), jnp.uint32)],
        scratch_shapes=[pltpu.VMEM((ROUTER_ROWS, LANES), F32),
                        pltpu.VMEM((tm, tm), BF16)],
        compiler_params=_cparams("arbitrary"),
        name="router",
    )(x, w, b)


def _dispatch_direct_kernel(fill_ref, pos_ref, xp_hbm, xs_hbm, zeros, sem, zsem):
    i = pl.program_id(0)
    n = pl.num_programs(0)
    tm = DISPATCH_TM
    slot = i % 2

    def step_wait(s):
        pltpu.make_async_copy(xp_hbm.at[pl.ds(0, tm * TOK_ROWS)], xs_hbm.at[pl.ds(0, tm * TOK_ROWS)],
                              sem.at[s]).wait()

    @pl.when(i == 0)
    def _():
        zeros[...] = jnp.zeros_like(zeros)
        for c in range(N_CLASSES):
            dst = pl.multiple_of(fill_ref[c] * TOK_ROWS, TOK_ROWS)
            pltpu.make_async_copy(zeros, xs_hbm.at[pl.ds(dst, MOE_TM * TOK_ROWS)], zsem.at[0]).start()
        for c in range(N_CLASSES):
            pltpu.make_async_copy(zeros, xs_hbm.at[pl.ds(0, MOE_TM * TOK_ROWS)], zsem.at[0]).wait()
        for k in range(MAX_IDLE_TILES):
            t = fill_ref[N_CLASSES] + k

            @pl.when(t < SORTED_TILES)
            def _():
                dst = pl.multiple_of(t * (MOE_TM * TOK_ROWS), MOE_TM * TOK_ROWS)
                tail = pltpu.make_async_copy(zeros, xs_hbm.at[pl.ds(dst, MOE_TM * TOK_ROWS)], zsem.at[0])
                tail.start()
                tail.wait()

    @pl.when(i >= 2)
    def _():
        step_wait(slot)

    first = i * (tm * TOK_ROWS)
    for r in range(tm):
        src = pl.multiple_of(first + r * TOK_ROWS, TOK_ROWS)
        dst = pl.multiple_of(pos_ref[0, r] * TOK_ROWS, TOK_ROWS)
        pltpu.make_async_copy(xp_hbm.at[pl.ds(src, TOK_ROWS)], xs_hbm.at[pl.ds(dst, TOK_ROWS)],
                              sem.at[slot]).start(priority=r % DMA_QUEUES)

    @pl.when(i == n - 1)
    def _():
        step_wait(1 - slot)
        step_wait(slot)


def _dispatch_direct_call(tables, xp):
    pos, fill_from = tables[0], tables[1]
    tm = DISPATCH_TM
    grid_spec = pltpu.PrefetchScalarGridSpec(
        num_scalar_prefetch=1,
        grid=(TOKENS // tm,),
        in_specs=[pl.BlockSpec((None, 1, tm), lambda i, f: (i, 0, 0), memory_space=pltpu.SMEM),
                  pl.BlockSpec(memory_space=pl.ANY)],
        out_specs=pl.BlockSpec(memory_space=pl.ANY),
        scratch_shapes=[pltpu.VMEM((MOE_TM * TOK_ROWS, LANES), jnp.uint32),
                        pltpu.SemaphoreType.DMA((2,)),
                        pltpu.SemaphoreType.DMA((1,))],
    )
    return pl.pallas_call(
        _dispatch_direct_kernel,
        grid_spec=grid_spec,
        out_shape=jax.ShapeDtypeStruct((SORTED_TOKENS * TOK_ROWS, LANES), jnp.uint32),
        compiler_params=_cparams("arbitrary"),
        name="dispatch",
    )(fill_from, pos, xp)


def _dispatch_call(tables, x, meta):
    pos, fill_from = tables[0], tables[1]
    tm = DISPATCH_TM
    grid_spec = pltpu.PrefetchScalarGridSpec(
        num_scalar_prefetch=1,
        grid=(TOKENS // tm,),
        in_specs=[pl.BlockSpec((None, 1, tm), lambda i, f: (i, 0, 0), memory_space=pltpu.SMEM),
                  pl.BlockSpec((tm, D_MODEL), lambda i, f: (i, 0)),
                  pl.BlockSpec((META_ROWS, tm), lambda i, f: (0, i))],
        out_specs=pl.BlockSpec(memory_space=pl.ANY),
        scratch_shapes=[pltpu.VMEM((2, tm * TOK_ROWS, LANES), jnp.uint32),
                        pltpu.VMEM((MOE_TM * TOK_ROWS, LANES), jnp.uint32),
                        pltpu.SemaphoreType.DMA((2,)),
                        pltpu.SemaphoreType.DMA((1,))],
    )
    return pl.pallas_call(
        _dispatch_kernel,
        grid_spec=grid_spec,
        out_shape=jax.ShapeDtypeStruct((SORTED_TOKENS * TOK_ROWS, LANES), jnp.uint32),
        compiler_params=_cparams("arbitrary"),
        name="dispatch",
    )(fill_from, pos, x, meta)


def _moe_kernel(in_ref, elo_ref, ehi_ref, flag_ref,
                xs_ref, w1a_ref, w3a_ref, w2a_ref, w1b_ref, w3b_ref, w2b_ref, ys_ref, w_up, w_down):
    j = pl.program_id(0)

    @pl.when((flag_ref[j] & FLAG_NEW_EXPERTS) != 0)
    def _():
        for n, ref in enumerate((w1a_ref, w3a_ref, w1b_ref, w3b_ref)):
            w_up[n] = ref[...].astype(BF16)
        for n, ref in enumerate((w2a_ref, w2b_ref)):
            w_down[n] = ref[...].astype(BF16)

    @pl.when((flag_ref[j] & FLAG_ACTIVE) != 0)
    def _():
        chunks = []
        for g in range(D_MODEL // (2 * LANES)):
            word = xs_ref[pl.ds(g, MOE_TM, stride=TOK_ROWS), :]
            lo = lax.bitcast_convert_type(lax.shift_left(word, jnp.uint32(16)), F32)
            hi = lax.bitcast_convert_type(word & jnp.uint32(HI_MASK), F32)
            chunks += [lo.astype(BF16), hi.astype(BF16)]
        xb = jnp.concatenate(chunks, axis=1)
        wts = lax.bitcast_convert_type(xs_ref[pl.ds(4, MOE_TM, stride=TOK_ROWS), :], F32)
        wa = wts[:, 0:1]
        wb = wts[:, 1:2]
        a1 = jnp.dot(xb, w_up[0], preferred_element_type=F32)
        a3 = jnp.dot(xb, w_up[1], preferred_element_type=F32)
        ha = ((a1 * jax.nn.sigmoid(a1)) * a3 * wa).astype(BF16)
        b1 = jnp.dot(xb, w_up[2], preferred_element_type=F32)
        b3 = jnp.dot(xb, w_up[3], preferred_element_type=F32)
        hb = ((b1 * jax.nn.sigmoid(b1)) * b3 * wb).astype(BF16)
        y = (jnp.dot(ha, w_down[0], preferred_element_type=F32)
             + jnp.dot(hb, w_down[1], preferred_element_type=F32))
        for k in range(N_LANE_CHUNKS):
            ys_ref[pl.ds(k, MOE_TM, stride=TOK_ROWS), :] = y[:, LANES * k:LANES * (k + 1)]

    @pl.when((flag_ref[j] & FLAG_ACTIVE) == 0)
    def _():
        ys_ref[...] = jnp.zeros_like(ys_ref)


def _moe_call(tables, xs, w1, w3, w2, layer):
    in_tile, e_lo, e_hi, flags = tables[2:]
    e_lo = e_lo + layer * N_EXPERTS
    e_hi = e_hi + layer * N_EXPERTS
    rows = MOE_TM * TOK_ROWS
    wspec = lambda shape, which: pl.BlockSpec(
        (None,) + shape, lambda j, ti, el, eh, ac: ((el, eh)[which][j], 0, 0))
    grid_spec = pltpu.PrefetchScalarGridSpec(
        num_scalar_prefetch=4,
        grid=(MOE_ITEMS,),
        in_specs=[pl.BlockSpec((rows, LANES), lambda j, ti, el, eh, ac: (ti[j], 0)),
                  wspec((D_MODEL, D_EXPERT), 0), wspec((D_MODEL, D_EXPERT), 0), wspec((D_EXPERT, D_MODEL), 0),
                  wspec((D_MODEL, D_EXPERT), 1), wspec((D_MODEL, D_EXPERT), 1), wspec((D_EXPERT, D_MODEL), 1)],
        out_specs=pl.BlockSpec((rows, LANES), lambda j, ti, el, eh, ac: (j, 0)),
        scratch_shapes=[pltpu.VMEM((4, D_MODEL, D_EXPERT), BF16),
                        pltpu.VMEM((2, D_EXPERT, D_MODEL), BF16)],
    )
    return pl.pallas_call(
        _moe_kernel,
        grid_spec=grid_spec,
        out_shape=jax.ShapeDtypeStruct((MOE_ITEMS * rows, LANES), F32),
        compiler_params=_cparams("arbitrary"),
        name="moe",
    )(in_tile, e_lo, e_hi, flags, xs, w1, w3, w2, w1, w3, w2)


def _ple_ln_kernel(pos_ref, pos_next_ref, x_ref, ys_hbm, p_ref, wpg_ref, bpg_ref, wp_ref, g_ref, b_ref,
                   o_ref, fbuf, sem):
    i = pl.program_id(0)
    n = pl.num_programs(0)
    tm = DISPATCH_TM
    slot = i % 2

    def row_gather(idx_ref, s):
        def body(h, carry):
            for queue in range(DMA_QUEUES):
                r = h * DMA_QUEUES + queue
                src = pl.multiple_of(idx_ref[0, r] * TOK_ROWS, TOK_ROWS)
                dst = pl.multiple_of(r * TOK_ROWS, TOK_ROWS)
                pltpu.make_async_copy(ys_hbm.at[pl.ds(src, TOK_ROWS)], fbuf.at[s, pl.ds(dst, TOK_ROWS)],
                                      sem.at[s]).start(priority=queue)
            return carry
        lax.fori_loop(0, tm // DMA_QUEUES, body, 0, unroll=4)

    def slab_wait(s):
        pltpu.make_async_copy(ys_hbm.at[pl.ds(0, tm * TOK_ROWS)], fbuf.at[s], sem.at[s]).wait()

    @pl.when(i == 0)
    def _():
        row_gather(pos_ref, slot)

    x = x_ref[...]
    xb = x.astype(BF16)
    n_blocks = D_MODEL // PLE_BN
    per_block = tm // n_blocks
    gate_blocks = []
    for c in range(n_blocks):
        for r in range(per_block * c, per_block * (c + 1)):
            src = pl.multiple_of(pos_next_ref[0, r] * TOK_ROWS, TOK_ROWS)
            pltpu.make_async_copy(ys_hbm.at[pl.ds(src, TOK_ROWS)],
                                  fbuf.at[1 - slot, pl.ds(r * TOK_ROWS, TOK_ROWS)],
                                  sem.at[1 - slot]).start(priority=r % DMA_QUEUES)
        cs = slice(PLE_BN * c, PLE_BN * (c + 1))
        gate_blocks.append(jax.nn.sigmoid(
            jnp.dot(xb, wpg_ref[:, cs], preferred_element_type=F32) + bpg_ref[:, cs]))
    gate = jnp.concatenate(gate_blocks, axis=1)
    pe = jnp.dot(p_ref[...].astype(BF16), wp_ref[...], preferred_element_type=F32)
    base = DN_ALPHA * x + gate * pe

    slab_wait(slot)

    @pl.when(i == n - 1)
    def _():
        slab_wait(1 - slot)

    ffn = jnp.concatenate([fbuf[slot, pl.ds(k, tm, stride=TOK_ROWS), :] for k in range(N_LANE_CHUNKS)], axis=1)
    o_ref[...] = _layer_norm(base + ffn, g_ref[...], b_ref[...])


def _ple_ln_call(pos, x, ys, p, wpg, bpg, wp, g, b):
    tm = DISPATCH_TM
    nt = TOKENS // tm
    row = lambda n: pl.BlockSpec((tm, n), lambda i: (i, 0))
    const = lambda *shape: pl.BlockSpec(shape, lambda i: (0,) * len(shape))
    return pl.pallas_call(
        _ple_ln_kernel,
        grid=(nt,),
        in_specs=[pl.BlockSpec((None, 1, tm), lambda i: (i, 0, 0), memory_space=pltpu.SMEM),
                  pl.BlockSpec((None, 1, tm), lambda i: (jnp.minimum(i + 1, nt - 1), 0, 0),
                               memory_space=pltpu.SMEM),
                  row(D_MODEL), pl.BlockSpec(memory_space=pl.ANY), row(D_PLE),
                  const(D_MODEL, D_MODEL), const(1, D_MODEL), const(D_PLE, D_MODEL),
                  const(1, D_MODEL), const(1, D_MODEL)],
        out_specs=row(D_MODEL),
        out_shape=jax.ShapeDtypeStruct((TOKENS, D_MODEL), F32),
        scratch_shapes=[pltpu.VMEM((2, tm * TOK_ROWS, LANES), F32),
                        pltpu.SemaphoreType.DMA((2,))],
        compiler_params=_cparams("arbitrary"),
        name="ple_ln",
    )(pos, pos, x, ys, p, wpg, bpg, wp, g, b)


def _dup_heads(w):
    lead = w.shape[:-1]
    w4 = w.reshape(lead + (N_KV_HEADS, 1, HEAD_DIM))
    return jnp.broadcast_to(w4, lead + (N_KV_HEADS, 2, HEAD_DIM)).reshape(lead + (2 * KV_CH,))


def _row(v):
    return v.reshape(1, -1)


def kernel(x, p, ln_emb_g, ln_emb_b, w_in, b_in, conv_w, conv_b, conv_ln_g, conv_ln_b, w_conv_out, w_attn_out, attn_sink, w_out, ln1_g, ln1_b, w_router_group, b_router_group, w_router_expert, b_router_expert, w1, w3, w2, w_p, w_pg, b_pg, ln2_g, ln2_b):
    c0, c1, c2, c3, c4 = (2 * CONV_CH, 2 * CONV_CH + D_MODEL, 2 * CONV_CH + D_MODEL + KV_CH,
                          2 * CONV_CH + D_MODEL + 2 * KV_CH, 2 * CONV_CH + 2 * D_MODEL + 2 * KV_CH)
    q_scale = HEAD_DIM ** -0.5 * LOG2E
    router_tm = 512

    w1_all = w1.reshape(DEPTH * N_EXPERTS, D_MODEL, D_EXPERT)
    w3_all = w3.reshape(DEPTH * N_EXPERTS, D_MODEL, D_EXPERT)
    w2_all = w2.reshape(DEPTH * N_EXPERTS, D_EXPERT, D_MODEL)
    xs = _ln_call(x.reshape(TOKENS, D_MODEL), _row(ln_emb_g), _row(ln_emb_b))
    for i in range(DEPTH):
        w, b = w_in[i], b_in[i]
        wq = (w[:, c0:c1] * q_scale).astype(BF16)
        bq = _row(b[c0:c1] * q_scale)
        wkv = jnp.concatenate([_dup_heads(w[:, c1:c2]), _dup_heads(w[:, c2:c3])], axis=1)
        bkv = jnp.concatenate([_dup_heads(b[c1:c2]), _dup_heads(b[c2:c3])])
        wgate = (w[:, c3:] * -LOG2E).astype(BF16)
        bgate = _row(b[c3:] * -LOG2E)
        blocks = lambda m: jnp.stack([m[..., :CONV_CH].reshape(m.shape[:-1] + (-1, GLU_BN)),
                                      m[..., CONV_CH:c0].reshape(m.shape[:-1] + (-1, GLU_BN)),
                                      m[..., c0:].reshape(m.shape[:-1] + (-1, GLU_BN))], axis=-2)
        w_glu = blocks(jnp.concatenate([w[:, :CONV_CH], w[:, CONV_CH:c0] * -LOG2E, wkv], axis=1))
        w_glu = w_glu.reshape(D_MODEL, -1).astype(BF16)
        b_glu = blocks(jnp.concatenate([b[:CONV_CH], b[CONV_CH:c0] * -LOG2E, bkv])).reshape(1, -1)
        cw16 = lax.bitcast_convert_type(jnp.pad(conv_w[i], ((0, 1), (0, 0))).astype(BF16), jnp.uint16)
        cw = cw16.astype(jnp.uint32) * jnp.uint32(0x00010001)
        cw = cw.reshape(32, N_LANE_CHUNKS, LANES).transpose(1, 0, 2)

        yc, kv = _glu_conv_call(xs, w_glu, b_glu, cw, _row(conv_b[i]))
        x1 = _mixer_call(attn_sink[i].astype(F32) * LOG2E, xs, yc, kv.reshape(BATCH, SEQ, 4 * KV_CH),
                         wq, bq, wgate, bgate, _row(conv_ln_g[i]), _row(conv_ln_b[i]),
                         w_conv_out[i].astype(BF16), w_attn_out[i].astype(BF16), w_out[i].astype(BF16),
                         _row(ln1_g[i]), _row(ln1_b[i]))

        wr = jnp.concatenate([w_router_group[i], w_router_expert[i]], axis=1).T
        wr = jnp.pad(wr, ((0, ROUTER_ROWS - wr.shape[0]), (0, 0)))
        wr_hi = wr.astype(BF16)
        wr_lo = (wr - wr_hi.astype(F32)).astype(BF16)
        br = jnp.pad(jnp.concatenate([b_router_group[i], b_router_expert[i]]), (0, ROUTER_ROWS - 20))
        br = jnp.broadcast_to(br[:, None], (ROUTER_ROWS, router_tm))
        meta, cnt, packed = _router_pack_call(x1, jnp.stack([wr_hi, wr_lo]), br, tm=router_tm)

        tables = _dispatch_tables(meta, cnt)
        sorted_x = _dispatch_direct_call(tables, packed)
        sorted_y = _moe_call(tables, sorted_x, w1_all, w3_all, w2_all, i)
        xs = _ple_ln_call(tables[0], x1, sorted_y, p[i].reshape(TOKENS, D_PLE), w_pg[i].astype(BF16),
                          _row(b_pg[i]), w_p[i].astype(BF16), _row(ln2_g[i]), _row(ln2_b[i]))
    return xs.reshape(BATCH, SEQ, D_MODEL)
```
